```python
import jax, jax.numpy as jnp
from jax import lax
import numpy as np

D_MODEL = 2048
BATCH = 2
SEQ = 16384
DEPTH = 1

CHUNK = 64
N_META = 16
MIX_DIM = D_MODEL
MLA_HEADS = 8
D_NOPE = 128
D_ROPE = 64
D_QK = D_NOPE + D_ROPE
D_V = 128
Q_RANK = 384
KV_RANK = 512
ROPE_THETA = 10000.0
ATTN_DIM = MLA_HEADS * D_V
Q_BLOCK = 128
CONV_DIM = MIX_DIM - ATTN_DIM
CONV_WIDTH = 3
IN_DIM = Q_RANK + KV_RANK + D_ROPE + 3 * CONV_DIM
PEER_HEADS = 8
N_KEYS = 128
N_EXPERTS = N_KEYS * N_KEYS
D_KEY = 256
D_HALF = D_KEY // 2
PEER_TOPK = 16
PEER_BLOCK = 128
DEEPNORM_ALPHA = (2.0 * DEPTH) ** 0.25
DEEPNORM_BETA = (8.0 * DEPTH) ** -0.25
EPS = 1e-5
NEG_INF = -1e30

kernel_name = "hybrid_mla_shortconv_peer_deepnorm"


def layer_norm(x, g, b):
    xf = x.astype(jnp.float32)
    mu = jnp.mean(xf, axis=-1, keepdims=True)
    var = jnp.mean(jnp.square(xf - mu), axis=-1, keepdims=True)
    return ((xf - mu) * lax.rsqrt(var + EPS) * g.astype(jnp.float32) + b.astype(jnp.float32)).astype(x.dtype)


def rms_norm(x, g):
    xf = x.astype(jnp.float32)
    r = lax.rsqrt(jnp.mean(jnp.square(xf), axis=-1, keepdims=True) + EPS)
    return (xf * r * g.astype(jnp.float32)).astype(x.dtype)


def rope_tables(n_pos, dtype):
    inv = 1.0 / (ROPE_THETA ** (jnp.arange(0, D_ROPE, 2, dtype=jnp.float32) / D_ROPE))
    ang = jnp.arange(n_pos, dtype=jnp.float32)[:, None] * inv[None, :]
    return jnp.cos(ang).astype(dtype), jnp.sin(ang).astype(dtype)


def apply_rope(x, cos, sin):
    x1, x2 = jnp.split(x, 2, axis=-1)
    return jnp.concatenate([x1 * cos - x2 * sin, x1 * sin + x2 * cos], axis=-1)


def chunk_ids(n_pos):
    p = jnp.arange(n_pos)
    return jnp.where(p < N_META, 0, (p - N_META) // CHUNK + 1)


def mla_scores(qn, qr, kn, kr):
    s = jnp.einsum('bqhd,bkhd->bhqk', qn, kn) + jnp.einsum('bqhd,bkd->bhqk', qr, kr)
    return s.astype(jnp.float32)


def mla_attention(c_q, c_kv, k_rope, q_norm_g, kv_norm_g, w_uq, w_ukv, cos, sin):
    B, L, _ = c_q.shape
    n_real = L - N_META
    q = (rms_norm(c_q, q_norm_g) @ w_uq).reshape(B, L, MLA_HEADS, D_QK)
    kv = (rms_norm(c_kv, kv_norm_g) @ w_ukv).reshape(B, L, MLA_HEADS, D_NOPE + D_V)
    k_nope, v = kv[..., :D_NOPE], kv[..., D_NOPE:]
    scale = D_QK ** -0.5
    q_nope = q[..., :D_NOPE] * scale
    q_rot = apply_rope(q[..., D_NOPE:], cos[:, None, :], sin[:, None, :]) * scale
    k_rot = apply_rope(k_rope, cos, sin)
    kchunk = chunk_ids(L)

    s_meta = mla_scores(q_nope[:, :N_META], q_rot[:, :N_META], k_nope[:, :N_META], k_rot[:, :N_META])
    p_meta = jax.nn.softmax(s_meta, axis=-1).astype(v.dtype)
    o_meta = jnp.einsum('bhqk,bkhd->bqhd', p_meta, v[:, :N_META])

    n_blk = n_real // Q_BLOCK
    def to_blocks(t):
        return t[:, N_META:].reshape(B, n_blk, Q_BLOCK, *t.shape[2:]).swapaxes(0, 1)

    def attend_block(args):
        qn_b, qr_b, blk = args
        s = mla_scores(qn_b, qr_b, k_nope, k_rot)
        qchunk = (blk * Q_BLOCK + jnp.arange(Q_BLOCK)) // CHUNK + 1
        visible = kchunk[None, :] <= qchunk[:, None]
        p = jax.nn.softmax(jnp.where(visible[None, None], s, NEG_INF), axis=-1).astype(v.dtype)
        return jnp.einsum('bhqk,bkhd->bqhd', p, v)

    o_real = lax.map(attend_block, (to_blocks(q_nope), to_blocks(q_rot), jnp.arange(n_blk)))
    o_real = o_real.swapaxes(0, 1).reshape(B, n_real, MLA_HEADS, D_V)
    o = jnp.concatenate([o_meta, o_real], axis=1)
    return o.reshape(B, L, ATTN_DIM)


def short_conv(b_gate, c_gate, h, conv_w):
    L = h.shape[1]
    u = jnp.pad(c_gate * h, ((0, 0), (CONV_WIDTH - 1, 0), (0, 0)))
    y = sum(u[:, j:j + L] * conv_w[j] for j in range(CONV_WIDTH))
    return b_gate * y


def peer_ffn(x, w_query, sub_keys, expert_u, expert_v):
    T = x.shape[0]
    n_blk = -(-T // PEER_BLOCK)
    xp = jnp.pad(x, ((0, n_blk * PEER_BLOCK - T), (0, 0))).reshape(n_blk, PEER_BLOCK, D_MODEL)

    def block(xb):
        q = (xb @ w_query).reshape(PEER_BLOCK, PEER_HEADS, 2, D_HALF)
        s = jnp.einsum('thcd,ckd->thck', q, sub_keys).astype(jnp.float32)
        top_s, top_i = lax.top_k(s, PEER_TOPK)
        cand = top_s[:, :, 0, :, None] + top_s[:, :, 1, None, :]
        cand = cand.reshape(PEER_BLOCK, PEER_HEADS, PEER_TOPK * PEER_TOPK)
        best_s, best_c = lax.top_k(cand, PEER_TOPK)
        i1 = jnp.take_along_axis(top_i[:, :, 0], best_c // PEER_TOPK, axis=-1)
        i2 = jnp.take_along_axis(top_i[:, :, 1], best_c % PEER_TOPK, axis=-1)
        idx = i1 * N_KEYS + i2
        g = jax.nn.softmax(best_s, axis=-1)
        u = expert_u[idx]
        a = jax.nn.gelu(jnp.einsum('thkd,td->thk', u, xb).astype(jnp.float32), approximate=False)
        w = (g * a).astype(xb.dtype)
        return jnp.einsum('thk,thkd->td', w, expert_v[idx])

    return lax.map(block, xp).reshape(n_blk * PEER_BLOCK, D_MODEL)[:T]


def setup_inputs(seed: int = 0) -> dict:
    key = jax.random.key(seed)
    ks = jax.random.split(key, 24)
    f32 = jnp.float32
    def nrm(k, shape, scale):
        return jax.random.normal(k, shape, f32) * scale
    def gain(k, shape):
        return 1.0 + 0.02 * jax.random.normal(k, shape, f32)
    return {
        "x": nrm(ks[0], (BATCH, SEQ, D_MODEL), 1.0),
        "meta_tokens": nrm(ks[1], (N_META, D_MODEL), 1.0),
        "ln0_g": gain(ks[2], (D_MODEL,)),
        "ln0_b": nrm(ks[3], (D_MODEL,), 0.02),
        "w_in": nrm(ks[4], (DEPTH, D_MODEL, IN_DIM), D_MODEL ** -0.5),
        "q_norm_g": gain(ks[5], (DEPTH, Q_RANK)),
        "kv_norm_g": gain(ks[6], (DEPTH, KV_RANK)),
        "w_uq": nrm(ks[7], (DEPTH, Q_RANK, MLA_HEADS * D_QK), Q_RANK ** -0.5),
        "w_ukv": nrm(ks[8], (DEPTH, KV_RANK, MLA_HEADS * (D_NOPE + D_V)), KV_RANK ** -0.5),
        "conv_w": nrm(ks[9], (DEPTH, CONV_WIDTH, CONV_DIM), CONV_WIDTH ** -0.5),
        "attn_out_g": gain(ks[10], (DEPTH, ATTN_DIM)),
        "conv_out_g": gain(ks[11], (DEPTH, CONV_DIM)),
        "w_o": nrm(ks[12], (DEPTH, MIX_DIM, D_MODEL), MIX_DIM ** -0.5 * DEEPNORM_BETA),
        "ln1_g": gain(ks[13], (DEPTH, D_MODEL)),
        "ln1_b": nrm(ks[14], (DEPTH, D_MODEL), 0.02),
        "peer_w_query": nrm(ks[15], (DEPTH, D_MODEL, PEER_HEADS * D_KEY), D_MODEL ** -0.5),
        "peer_sub_keys": nrm(ks[16], (DEPTH, 2, N_KEYS, D_HALF), D_HALF ** -0.5),
        "peer_u": nrm(ks[17], (DEPTH, N_EXPERTS, D_MODEL), D_MODEL ** -0.5),
        "peer_v": nrm(ks[18], (DEPTH, N_EXPERTS, D_MODEL), DEEPNORM_BETA),
        "ln2_g": gain(ks[19], (DEPTH, D_MODEL)),
        "ln2_b": nrm(ks[20], (DEPTH, D_MODEL), 0.02),
    }


def reference(x, meta_tokens, ln0_g, ln0_b, w_in, q_norm_g, kv_norm_g, w_uq, w_ukv, conv_w,
              attn_out_g, conv_out_g, w_o, ln1_g, ln1_b, peer_w_query, peer_sub_keys,
              peer_u, peer_v, ln2_g, ln2_b):
    B = x.shape[0]
    meta = jnp.broadcast_to(meta_tokens[None].astype(x.dtype), (B, N_META, D_MODEL))
    h = layer_norm(jnp.concatenate([meta, x], axis=1), ln0_g, ln0_b)
    L = h.shape[1]
    cos, sin = rope_tables(L, h.dtype)
    splits = np.cumsum([Q_RANK, KV_RANK, D_ROPE, CONV_DIM, CONV_DIM]).tolist()
    for layer in range(DEPTH):
        proj = h @ w_in[layer]
        c_q, c_kv, k_rope, b_gate, c_gate, h_conv = jnp.split(proj, splits, axis=-1)
        o_attn = mla_attention(c_q, c_kv, k_rope, q_norm_g[layer], kv_norm_g[layer],
                               w_uq[layer], w_ukv[layer], cos, sin)
        o_conv = short_conv(b_gate, c_gate, h_conv, conv_w[layer])
        mixed = jnp.concatenate([rms_norm(o_attn, attn_out_g[layer]),
                                 rms_norm(o_conv, conv_out_g[layer])], axis=-1) @ w_o[layer]
        h = layer_norm(DEEPNORM_ALPHA * h + mixed, ln1_g[layer], ln1_b[layer])
        f = peer_ffn(h.reshape(B * L, D_MODEL), peer_w_query[layer], peer_sub_keys[layer],
                     peer_u[layer], peer_v[layer]).reshape(B, L, D_MODEL)
        h = layer_norm(DEEPNORM_ALPHA * h + f, ln2_g[layer], ln2_b[layer])
    return h[:, N_META:]
```

```python
import functools

import jax
import jax.numpy as jnp
import numpy as np
from jax import lax
from jax.experimental import pallas as pl
from jax.experimental.pallas import tpu as pltpu

CHUNK = 64
N_META = 16
MLA_HEADS = 8
D_NOPE = 128
D_ROPE = 64
D_QK = D_NOPE + D_ROPE
D_V = 128
Q_RANK = 384
KV_RANK = 512
ROPE_THETA = 10000.0
ATTN_DIM = MLA_HEADS * D_V
CONV_WIDTH = 3
PEER_HEADS = 8
N_KEYS = 128
D_HALF = 128
PEER_TOPK = 16
DEPTH = 1
DEEPNORM_ALPHA = (2.0 * DEPTH) ** 0.25
EPS = 1e-5
NEG_INF = -1e30

_CHUNK_SHIFT = CHUNK.bit_length() - 1
_TOPK_SHIFT = PEER_TOPK.bit_length() - 1
assert 1 << _CHUNK_SHIFT == CHUNK and 1 << _TOPK_SHIFT == PEER_TOPK

LANES = 128
QK_PAD = 256
VMEM_LIMIT = 58 * 1024 * 1024

F32 = jnp.float32
BF16 = jnp.bfloat16


def _dot(a, b):
    return jnp.dot(a, b, preferred_element_type=F32)


def _dot_nt(a, b):
    return lax.dot_general(a, b, (((1,), (1,)), ((), ())), preferred_element_type=F32)


def _layer_norm(x, g, b):
    mu = jnp.mean(x, axis=-1, keepdims=True)
    xc = x - mu
    var = jnp.mean(xc * xc, axis=-1, keepdims=True)
    return xc * lax.rsqrt(var + EPS) * g + b


def _rms_norm(x, g):
    return x * lax.rsqrt(jnp.mean(x * x, axis=-1, keepdims=True) + EPS) * g


def _const_spec(shape):
    nd = len(shape)
    return pl.BlockSpec(shape, lambda *_: (0,) * nd, pipeline_mode=pl.Buffered(1))


_OFF_CQ = 0
_OFF_CKV = _OFF_CQ + Q_RANK
_OFF_KRA = _OFF_CKV + KV_RANK
_OFF_KRB = _OFF_KRA + LANES
_OFF_BG = _OFF_KRB + LANES


def _inproj_kernel(x_ref, t1_ref, t2_ref, uinit_ref, ln0g_ref, ln0b_ref, win_ref, qg_ref, kvg_ref,
                   wqa_ref, wqb_ref, wk_ref, wv_ref, convw_ref, convg_ref,
                   q_out, k_out, v_out, convn_out, utail_out, ubuf, *, tm, conv_dim):
    i = pl.program_id(1)
    off_cg = _OFF_BG + conv_dim
    off_hc = off_cg + conv_dim

    @pl.when(i == 0)
    def _():
        ubuf[0:8, :] = uinit_ref[...]

    hb = _layer_norm(x_ref[...], ln0g_ref[...], ln0b_ref[...]).astype(BF16)
    t1 = t1_ref[...]
    t2 = t2_ref[...]
    scale = D_QK ** -0.5

    cq = _dot(hb, win_ref[:, _OFF_CQ:_OFF_CQ + Q_RANK])
    ckv = _dot(hb, win_ref[:, _OFF_CKV:_OFF_CKV + KV_RANK])
    kra = _dot(hb, win_ref[:, _OFF_KRA:_OFF_KRA + LANES])
    krb = _dot(hb, win_ref[:, _OFF_KRB:_OFF_KRB + LANES])
    krot = (kra * t1 + krb * t2).astype(BF16)
    cqn = _rms_norm(cq, qg_ref[...]).astype(BF16)
    ckvn = _rms_norm(ckv, kvg_ref[...]).astype(BF16)

    for h in range(MLA_HEADS):
        qa = _dot(cqn, wqa_ref[:, h * QK_PAD:(h + 1) * QK_PAD])
        qb = _dot(cqn, wqb_ref[:, h * LANES:(h + 1) * LANES])
        q_out[h, :, 0:LANES] = (qa[:, 0:LANES] * scale).astype(BF16)
        q_out[h, :, LANES:QK_PAD] = ((qa[:, LANES:QK_PAD] * t1 + qb * t2) * scale).astype(BF16)
        k_out[h, :, 0:LANES] = _dot(ckvn, wk_ref[:, h * LANES:(h + 1) * LANES]).astype(BF16)
        k_out[h, :, LANES:QK_PAD] = krot
        v_out[h] = _dot(ckvn, wv_ref[:, h * LANES:(h + 1) * LANES]).astype(BF16)

    bg = _dot(hb, win_ref[:, _OFF_BG:_OFF_BG + conv_dim])
    cg = _dot(hb, win_ref[:, off_cg:off_cg + conv_dim])
    hc = _dot(hb, win_ref[:, off_hc:off_hc + conv_dim])
    u = cg * hc
    ubuf[8:8 + tm, :] = u
    u1 = ubuf[7:7 + tm, :]
    u2 = ubuf[6:6 + tm, :]
    w = convw_ref[...]
    y = bg * (u2 * w[0:1, :] + u1 * w[1:2, :] + u * w[2:3, :])
    convn_out[...] = _rms_norm(y, convg_ref[...]).astype(BF16)
    tail = ubuf[tm:tm + 8, :]
    utail_out[...] = tail
    ubuf[0:8, :] = tail


def _inproj(x, t1, t2, uinit, ln0g, ln0b, win, qg, kvg, wqa, wqb, wk, wv, convw, convg, *, tm):
    b, s, d = x.shape
    conv_dim = convw.shape[1]
    grid = (b, s // tm)
    h = MLA_HEADS
    out_shape = (
        jax.ShapeDtypeStruct((b, h, s, QK_PAD), BF16),
        jax.ShapeDtypeStruct((b, h, s, QK_PAD), BF16),
        jax.ShapeDtypeStruct((b, h, s, D_V), BF16),
        jax.ShapeDtypeStruct((b, s, conv_dim), BF16),
        jax.ShapeDtypeStruct((b, 8, conv_dim), F32),
    )
    in_specs = [
        pl.BlockSpec((None, tm, d), lambda bi, i: (bi, i, 0)),
        pl.BlockSpec((tm, LANES), lambda bi, i: (i, 0)),
        pl.BlockSpec((tm, LANES), lambda bi, i: (i, 0)),
        _const_spec(uinit.shape), _const_spec(ln0g.shape), _const_spec(ln0b.shape), _const_spec(win.shape),
        _const_spec(qg.shape), _const_spec(kvg.shape), _const_spec(wqa.shape), _const_spec(wqb.shape),
        _const_spec(wk.shape), _const_spec(wv.shape), _const_spec(convw.shape), _const_spec(convg.shape),
    ]
    out_specs = (
        pl.BlockSpec((None, h, tm, QK_PAD), lambda bi, i: (bi, 0, i, 0)),
        pl.BlockSpec((None, h, tm, QK_PAD), lambda bi, i: (bi, 0, i, 0)),
        pl.BlockSpec((None, h, tm, D_V), lambda bi, i: (bi, 0, i, 0)),
        pl.BlockSpec((None, tm, conv_dim), lambda bi, i: (bi, i, 0)),
        pl.BlockSpec((None, 8, conv_dim), lambda bi, i: (bi, 0, 0)),
    )
    return pl.pallas_call(
        functools.partial(_inproj_kernel, tm=tm, conv_dim=conv_dim),
        grid=grid, in_specs=in_specs, out_specs=out_specs, out_shape=out_shape,
        scratch_shapes=[pltpu.VMEM((tm + 8, conv_dim), F32)],
        compiler_params=pltpu.CompilerParams(
            dimension_semantics=("arbitrary", "arbitrary"), vmem_limit_bytes=VMEM_LIMIT),
        name="inproj",
    )(x, t1, t2, uinit, ln0g, ln0b, win, qg, kvg, wqa, wqb, wk, wv, convw, convg)


def _attn_kernel(q_ref, k_ref, v_ref, km_ref, vm_ref, o_ref, *, tq, tk):
    i = pl.program_id(2)
    q = q_ref[...]
    nsub = tq // tk

    def update(carry, s, v):
        m, l, acc = carry
        mn = jnp.maximum(m, jnp.max(s, axis=1, keepdims=True))
        a = jnp.exp(m - mn)
        p = jnp.exp(s - mn)
        l = a * l + jnp.sum(p, axis=1, keepdims=True)
        acc = a * acc + _dot(p.astype(BF16), v)
        return mn, l, acc

    s = _dot_nt(q, km_ref[...])
    col = lax.broadcasted_iota(jnp.int32, s.shape, 1)
    s = jnp.where(col < N_META, s, NEG_INF)
    m0 = jnp.max(s, axis=1, keepdims=True)
    p0 = jnp.exp(s - m0)
    carry = (m0, jnp.sum(p0, axis=1, keepdims=True), _dot(p0.astype(BF16), vm_ref[...]))

    def body(j, carry):
        start = pl.multiple_of(j * tk, tk)
        k = k_ref[pl.ds(start, tk), :]
        v = v_ref[pl.ds(start, tk), :]
        return update(carry, _dot_nt(q, k), v)

    carry = lax.fori_loop(0, i * nsub, body, carry)

    row_chunk = lax.broadcasted_iota(jnp.int32, (tq, tk), 0) >> _CHUNK_SHIFT
    col_chunk = lax.broadcasted_iota(jnp.int32, (tq, tk), 1) >> _CHUNK_SHIFT
    for jj in range(nsub):
        start = pl.multiple_of((i * nsub + jj) * tk, tk)
        k = k_ref[pl.ds(start, tk), :]
        v = v_ref[pl.ds(start, tk), :]
        s = _dot_nt(q, k)
        s = jnp.where(col_chunk + (jj * tk) // CHUNK <= row_chunk, s, NEG_INF)
        carry = update(carry, s, v)

    m, l, acc = carry
    o_ref[...] = (acc / l).astype(BF16)


def _attention(q, k, v, km, vm, *, tq, tk):
    b, h, s, _ = q.shape
    grid = (b, h, s // tq)
    return pl.pallas_call(
        functools.partial(_attn_kernel, tq=tq, tk=tk),
        grid=grid,
        in_specs=[
            pl.BlockSpec((None, None, tq, QK_PAD), lambda bi, hi, i: (bi, hi, i, 0)),
            pl.BlockSpec((None, None, s, QK_PAD), lambda bi, hi, i: (bi, hi, 0, 0)),
            pl.BlockSpec((None, None, s, D_V), lambda bi, hi, i: (bi, hi, 0, 0)),
            pl.BlockSpec((None, LANES, QK_PAD), lambda bi, hi, i: (hi, 0, 0)),
            pl.BlockSpec((None, LANES, D_V), lambda bi, hi, i: (hi, 0, 0)),
        ],
        out_specs=pl.BlockSpec((None, tq, D_V), lambda bi, hi, i: (bi, i, hi)),
        out_shape=jax.ShapeDtypeStruct((b, s, h * D_V), BF16),
        compiler_params=pltpu.CompilerParams(
            dimension_semantics=("arbitrary", "arbitrary", "arbitrary"), vmem_limit_bytes=VMEM_LIMIT),
        name="attn",
    )(q, k, v, km, vm)


def _outproj_kernel(x_ref, oa_ref, cn_ref, ln0g_ref, ln0b_ref, ag_ref, wo_ref, ln1g_ref, ln1b_ref,
                    wq_ref, sk_ref, h1_out, h1b_out, st_out):
    h0 = _layer_norm(x_ref[...], ln0g_ref[...], ln0b_ref[...])
    an = _rms_norm(oa_ref[...].astype(F32), ag_ref[...]).astype(BF16)
    mixed = _dot(an, wo_ref[0:ATTN_DIM, :]) + _dot(cn_ref[...], wo_ref[ATTN_DIM:, :])
    h1 = _layer_norm(DEEPNORM_ALPHA * h0 + mixed, ln1g_ref[...], ln1b_ref[...])
    h1_out[...] = h1
    h1b = h1.astype(BF16)
    h1b_out[...] = h1b
    pq = _dot(h1b, wq_ref[...])
    for hc in range(PEER_HEADS * 2):
        pqs = pq[:, hc * D_HALF:(hc + 1) * D_HALF].astype(BF16)
        st_out[hc * N_KEYS:(hc + 1) * N_KEYS, :] = _dot_nt(sk_ref[hc % 2], pqs)


def _outproj(x, oa, cn, ln0g, ln0b, ag, wo, ln1g, ln1b, wq, sk, *, tm):
    b, s, d = x.shape
    grid = (b, s // tm)
    nsc = PEER_HEADS * 2 * N_KEYS
    tok = lambda w: pl.BlockSpec((None, tm, w), lambda bi, i: (bi, i, 0))
    return pl.pallas_call(
        _outproj_kernel,
        grid=grid,
        in_specs=[tok(d), tok(oa.shape[2]), tok(cn.shape[2]),
                  _const_spec(ln0g.shape), _const_spec(ln0b.shape), _const_spec(ag.shape), _const_spec(wo.shape),
                  _const_spec(ln1g.shape), _const_spec(ln1b.shape), _const_spec(wq.shape), _const_spec(sk.shape)],
        out_specs=(tok(d), tok(d), pl.BlockSpec((None, nsc, tm), lambda bi, i: (bi, 0, i))),
        out_shape=(jax.ShapeDtypeStruct((b, s, d), F32), jax.ShapeDtypeStruct((b, s, d), BF16),
                   jax.ShapeDtypeStruct((b, nsc, s), F32)),
        compiler_params=pltpu.CompilerParams(
            dimension_semantics=("arbitrary", "arbitrary"), vmem_limit_bytes=VMEM_LIMIT),
        name="outproj",
    )(x, oa, cn, ln0g, ln0b, ag, wo, ln1g, ln1b, wq, sk)


_BIG_ID = 1 << 20


def _extract_top(s, ids, n):
    w = s.shape[1]
    rows = lax.broadcasted_iota(jnp.int32, (n, w), 0)
    vals = jnp.zeros((n, w), F32)
    sel = jnp.zeros((n, w), jnp.int32)
    val_rows = []
    for r in range(n):
        m = jnp.max(s, axis=0, keepdims=True)
        pick = jnp.min(jnp.where(s == m, ids, _BIG_ID), axis=0, keepdims=True)
        vals = jnp.where(rows == r, m, vals)
        sel = jnp.where(rows == r, pick, sel)
        val_rows.append(m)
        s = jnp.where(ids == pick, -jnp.inf, s)
    return vals, sel, val_rows


def _route_kernel(st_ref, i1_out, i2_out, g_out):
    w = st_ref.shape[1]
    k = PEER_TOPK
    key_ids = lax.broadcasted_iota(jnp.int32, (N_KEYS, w), 0)
    r16 = lax.broadcasted_iota(jnp.int32, (k, w), 0)
    r8 = lax.broadcasted_iota(jnp.int32, (8, w), 0)

    def head(h, _):
        base = pl.multiple_of(h * 2 * N_KEYS, 2 * N_KEYS)
        v1, i1, v1rows = _extract_top(st_ref[pl.ds(base, N_KEYS), :], key_ids, k)
        v2, i2, v2rows = _extract_top(st_ref[pl.ds(base + N_KEYS, N_KEYS), :], key_ids, k)
        pieces, piece_ids = [], []

        def add(vals, ids, ok):
            pieces.append(jnp.where(ok, vals, -jnp.inf))
            piece_ids.append(ids)

        add(v1rows[0] + v2, r16, r16 < k)
        for a in (1, 2, 3):
            add(v1rows[a] + v2[0:8, :], a * k + r8, (a + 1) * (r8 + 1) <= k)
        add(v1 + v2rows[0], r16 * k, r16 >= 4)
        for bb in (1, 2):
            add(v1[0:8, :] + v2rows[bb], r8 * k + bb, (r8 >= 4) & ((r8 + 1) * (bb + 1) <= k))
        cand = jnp.concatenate(pieces, axis=0)
        cand_ids = jnp.concatenate(piece_ids, axis=0)
        best, best_id, _ = _extract_top(cand, cand_ids, k)
        a_sel = best_id >> _TOPK_SHIFT
        b_sel = best_id & (k - 1)
        e1 = jnp.zeros((k, w), jnp.int32)
        e2 = jnp.zeros((k, w), jnp.int32)
        for c in range(k):
            e1 = jnp.where(a_sel == c, i1[c:c + 1, :], e1)
            e2 = jnp.where(b_sel == c, i2[c:c + 1, :], e2)
        ex = jnp.exp(best - best[0:1, :])
        gate = ex / jnp.sum(ex, axis=0, keepdims=True)
        o = pl.multiple_of(h * k, k)
        i1_out[pl.ds(o, k), :] = e1
        i2_out[pl.ds(o, k), :] = e2
        g_out[pl.ds(o, k), :] = gate
        return 0

    lax.fori_loop(0, PEER_HEADS, head, 0)


def _route(st, *, tl):
    b, nsc, s = st.shape
    nj = PEER_HEADS * PEER_TOPK
    grid = (b, s // tl)
    ospec = pl.BlockSpec((None, nj, tl), lambda bi, i: (bi, 0, i))
    return pl.pallas_call(
        _route_kernel,
        grid=grid,
        in_specs=[pl.BlockSpec((None, nsc, tl), lambda bi, i: (bi, 0, i))],
        out_specs=(ospec, ospec, ospec),
        out_shape=(jax.ShapeDtypeStruct((b, nj, s), jnp.int32), jax.ShapeDtypeStruct((b, nj, s), jnp.int32),
                   jax.ShapeDtypeStruct((b, nj, s), F32)),
        compiler_params=pltpu.CompilerParams(dimension_semantics=("arbitrary", "arbitrary")),
        name="route",
    )(st)


_GROUP = 16
_PITCH = N_KEYS + 4


def _scatter_kernel(i1_ref, i2_ref, g_ref, gd_out, stage):
    tg = i1_ref.shape[0]
    nj = i1_ref.shape[1]
    sub = lax.broadcasted_iota(jnp.int32, (N_KEYS, nj), 0)

    def group(gi, _):
        t0 = pl.multiple_of(gi * _GROUP, _GROUP)
        for t in range(_GROUP):
            r1 = i1_ref[pl.ds(t0 + t, 1), :]
            r2 = i2_ref[pl.ds(t0 + t, 1), :]
            gg = g_ref[pl.ds(t0 + t, 1), :]
            p1 = jnp.where(sub == r1, gg, 0.0).astype(BF16)
            p2 = jnp.where(sub == r2, 1.0, 0.0).astype(BF16)
            stage[t * _PITCH:t * _PITCH + N_KEYS, :] = _dot_nt(p1, p2)
        for a in range(N_KEYS):
            rows = stage[pl.ds(a, _GROUP, stride=_PITCH), :]
            gd_out[pl.ds(t0, _GROUP), a * N_KEYS:(a + 1) * N_KEYS] = rows.astype(BF16)
        return 0

    lax.fori_loop(0, tg // _GROUP, group, 0)


def _scatter(i1, i2, g, *, tg):
    t, nj = i1.shape
    ne = N_KEYS * N_KEYS
    spec = pl.BlockSpec((tg, nj), lambda i: (i, 0))
    return pl.pallas_call(
        _scatter_kernel,
        grid=(t // tg,),
        in_specs=[spec, spec, spec],
        out_specs=pl.BlockSpec((tg, ne), lambda i: (i, 0)),
        out_shape=jax.ShapeDtypeStruct((t, ne), BF16),
        scratch_shapes=[pltpu.VMEM((_GROUP * _PITCH, N_KEYS), F32)],
        compiler_params=pltpu.CompilerParams(dimension_semantics=("arbitrary",)),
        name="scatter",
    )(i1, i2, g)


def _peer_kernel(xb_ref, h1_ref, ut_ref, v_ref, gd_ref, ln2g_ref, ln2b_ref, o_ref, acc):
    e = pl.program_id(1)

    @pl.when(e == 0)
    def _():
        acc[...] = jnp.zeros_like(acc)

    a = _dot(xb_ref[...], ut_ref[...])
    act = 0.5 * a * (1.0 + lax.erf(a * (2.0 ** -0.5)))
    hd = (act * gd_ref[...].astype(F32)).astype(BF16)
    acc[...] += _dot(hd, v_ref[...])

    @pl.when(e == pl.num_programs(1) - 1)
    def _():
        o_ref[...] = _layer_norm(DEEPNORM_ALPHA * h1_ref[...] + acc[...], ln2g_ref[...], ln2b_ref[...])


def _peer(xb, h1, ut, v, gd, ln2g, ln2b, *, tm, ec):
    t, d = xb.shape
    ne = v.shape[0]
    grid = (t // tm, ne // ec)
    return pl.pallas_call(
        _peer_kernel,
        grid=grid,
        in_specs=[
            pl.BlockSpec((tm, d), lambda i, e: (i, 0)),
            pl.BlockSpec((tm, d), lambda i, e: (i, 0)),
            pl.BlockSpec((d, ec), lambda i, e: (0, e)),
            pl.BlockSpec((ec, d), lambda i, e: (e, 0)),
            pl.BlockSpec((tm, ec), lambda i, e: (i, e)),
            _const_spec(ln2g.shape), _const_spec(ln2b.shape),
        ],
        out_specs=pl.BlockSpec((tm, d), lambda i, e: (i, 0)),
        out_shape=jax.ShapeDtypeStruct((t, d), F32),
        scratch_shapes=[pltpu.VMEM((tm, d), F32)],
        compiler_params=pltpu.CompilerParams(
            dimension_semantics=("arbitrary", "arbitrary"), vmem_limit_bytes=VMEM_LIMIT),
        name="peer",
    )(xb, h1, ut, v, gd, ln2g, ln2b)


def _rope_tables(first_pos, n):
    inv = 1.0 / (ROPE_THETA ** (jnp.arange(0, D_ROPE, 2, dtype=F32) / D_ROPE))
    ang = (first_pos + jnp.arange(n, dtype=F32))[:, None] * inv[None, :]
    cos, sin = jnp.cos(ang), jnp.sin(ang)
    z = jnp.zeros((n, LANES - D_ROPE), F32)
    return jnp.concatenate([cos, cos, z], axis=1), jnp.concatenate([-sin, sin, z], axis=1)


def _swap_halves(w):
    half = w.shape[-1] // 2
    return jnp.concatenate([w[..., half:], w[..., :half]], axis=-1)


def _pick(n, prefs):
    for p in prefs:
        if n % p == 0:
            return p
    raise ValueError(f"no tile in {prefs} divides {n}")


def kernel(x, meta_tokens, ln0_g, ln0_b, w_in, q_norm_g, kv_norm_g, w_uq, w_ukv, conv_w, attn_out_g, conv_out_g,
           w_o, ln1_g, ln1_b, peer_w_query, peer_sub_keys, peer_u, peer_v, ln2_g, ln2_b):
    b, s, d = x.shape
    assert w_in.shape[0] == 1 and s % CHUNK == 0
    conv_dim = conv_w.shape[2]
    h = MLA_HEADS
    row = lambda a: a.reshape(1, -1).astype(F32)

    wi = w_in[0]
    o_kv, o_kr, o_b = Q_RANK, Q_RANK + KV_RANK, Q_RANK + KV_RANK + D_ROPE
    w_kr = wi[:, o_kr:o_b]
    zpad = jnp.zeros((d, LANES - D_ROPE), F32)
    win = jnp.concatenate(
        [wi[:, :o_kr], w_kr, zpad, _swap_halves(w_kr), zpad, wi[:, o_b:]], axis=1).astype(BF16)
    wq3 = w_uq[0].reshape(Q_RANK, h, D_QK)
    zq = jnp.zeros((Q_RANK, h, QK_PAD - D_QK), F32)
    wqa = jnp.concatenate([wq3, zq], axis=2).reshape(Q_RANK, h * QK_PAD).astype(BF16)
    wqb = jnp.concatenate([_swap_halves(wq3[:, :, D_NOPE:]), zq], axis=2).reshape(Q_RANK, h * LANES).astype(BF16)
    wkv3 = w_ukv[0].reshape(KV_RANK, h, D_NOPE + D_V)
    wk = wkv3[:, :, :D_NOPE].reshape(KV_RANK, h * D_NOPE).astype(BF16)
    wv = wkv3[:, :, D_NOPE:].reshape(KV_RANK, h * D_V).astype(BF16)
    wo = w_o[0].astype(BF16)
    wpq = peer_w_query[0].astype(BF16)
    sk = peer_sub_keys[0].astype(BF16)
    ut = peer_u[0].T.astype(BF16)
    pv = peer_v[0].astype(BF16)

    inproj_w = (row(ln0_g), row(ln0_b), win, row(q_norm_g[0]), row(kv_norm_g[0]), wqa, wqb, wk, wv,
                conv_w[0].astype(F32), row(conv_out_g[0]))

    t1m, t2m = _rope_tables(0, N_META)
    _, km, vm, _, utail = _inproj(meta_tokens[None].astype(F32), t1m, t2m, jnp.zeros((8, conv_dim), F32),
                                  *inproj_w, tm=N_META)
    km = jnp.pad(km[0], ((0, 0), (0, LANES - N_META), (0, 0)))
    vm = jnp.pad(vm[0], ((0, 0), (0, LANES - N_META), (0, 0)))

    t1, t2 = _rope_tables(N_META, s)
    q, k, v, convn, _ = _inproj(x, t1, t2, utail[0], *inproj_w, tm=_pick(s, (512, 256, 128, 64)))
    tq = _pick(s, (512, 256, 128, 64))
    oa = _attention(q, k, v, km, vm, tq=tq, tk=tq)
    h1, h1b, st = _outproj(x, oa, convn, row(ln0_g), row(ln0_b), row(attn_out_g[0]), wo, row(ln1_g[0]),
                           row(ln1_b[0]), wpq, sk, tm=_pick(s, (256, 128)))
    i1, i2, g = _route(st, tl=_pick(s, (256, 128)))
    tok = lambda a: jnp.swapaxes(a, 1, 2).reshape(b * s, a.shape[1])
    gd = _scatter(tok(i1), tok(i2), tok(g), tg=_pick(b * s, (64, 32, 16)))
    out = _peer(h1b.reshape(b * s, d), h1.reshape(b * s, d), ut, pv, gd, row(ln2_g[0]), row(ln2_b[0]),
                tm=_pick(b * s, (512, 256, 128)), ec=1024)
    return out.reshape(b, s, d)
```

```python
import functools

import jax
import jax.numpy as jnp
import numpy as np
from jax import lax
from jax.experimental import pallas as pl
from jax.experimental.pallas import tpu as pltpu

CHUNK = 64
N_META = 16
MLA_HEADS = 8
D_NOPE = 128
D_ROPE = 64
D_QK = D_NOPE + D_ROPE
D_V = 128
Q_RANK = 384
KV_RANK = 512
ROPE_THETA = 10000.0
ATTN_DIM = MLA_HEADS * D_V
CONV_WIDTH = 3
PEER_HEADS = 8
N_KEYS = 128
D_HALF = 128
PEER_TOPK = 16
DEPTH = 1
DEEPNORM_ALPHA = (2.0 * DEPTH) ** 0.25
EPS = 1e-5
NEG_INF = -1e30

_CHUNK_SHIFT = CHUNK.bit_length() - 1
_TOPK_SHIFT = PEER_TOPK.bit_length() - 1
assert 1 << _CHUNK_SHIFT == CHUNK and 1 << _TOPK_SHIFT == PEER_TOPK

LANES = 128
QK_PAD = 256
_KV_BLOCK = 512
_Q_BLOCK = 256
_Q_TILE = 2 * _KV_BLOCK
_LOG2E = 1.4426950408889634
VMEM_LIMIT = 58 * 1024 * 1024

F32 = jnp.float32
BF16 = jnp.bfloat16


def _dot(a, b):
    return jnp.dot(a, b, preferred_element_type=F32)


def _dot_nt(a, b):
    return lax.dot_general(a, b, (((1,), (1,)), ((), ())), preferred_element_type=F32)


def _layer_norm(x, g, b):
    mu = jnp.mean(x, axis=-1, keepdims=True)
    xc = x - mu
    var = jnp.mean(xc * xc, axis=-1, keepdims=True)
    return xc * lax.rsqrt(var + EPS) * g + b


def _rms_norm(x, g):
    return x * lax.rsqrt(jnp.mean(x * x, axis=-1, keepdims=True) + EPS) * g


def _const_spec(shape):
    nd = len(shape)
    return pl.BlockSpec(shape, lambda *_: (0,) * nd, pipeline_mode=pl.Buffered(1))


_OFF_CQ = 0
_OFF_CKV = _OFF_CQ + Q_RANK
_OFF_KRA = _OFF_CKV + KV_RANK
_OFF_KRB = _OFF_KRA + LANES
_OFF_BG = _OFF_KRB + LANES


def _inproj_kernel(x_ref, t1_ref, t2_ref, t1t_ref, t2t_ref, uinit_ref, ln0g_ref, ln0b_ref, win_ref, qg_ref,
                   kvg_ref, wqat_ref, wqbt_ref, wk_ref, wvt_ref, convw_ref, convg_ref,
                   qt_out, k_out, vt_out, convn_out, utail_out, ubuf, *, tm, vb, conv_dim):
    i = pl.program_id(1)
    off_cg = _OFF_BG + conv_dim
    off_hc = off_cg + conv_dim

    @pl.when(i == 0)
    def _():
        ubuf[0:8, :] = uinit_ref[...]

    hb = _layer_norm(x_ref[...], ln0g_ref[...], ln0b_ref[...]).astype(BF16)
    t1 = t1_ref[...]
    t2 = t2_ref[...]
    t1t = t1t_ref[...]
    t2t = t2t_ref[...]
    qscale = D_QK ** -0.5 * _LOG2E

    cq = _dot(hb, win_ref[:, _OFF_CQ:_OFF_CQ + Q_RANK])
    ckv = _dot(hb, win_ref[:, _OFF_CKV:_OFF_CKV + KV_RANK])
    kra = _dot(hb, win_ref[:, _OFF_KRA:_OFF_KRA + LANES])
    krb = _dot(hb, win_ref[:, _OFF_KRB:_OFF_KRB + LANES])
    krot = (kra * t1 + krb * t2).astype(BF16)
    cqn = _rms_norm(cq, qg_ref[...]).astype(BF16)
    ckvn = _rms_norm(ckv, kvg_ref[...]).astype(BF16)

    for h in range(MLA_HEADS):
        qat = _dot_nt(wqat_ref[h * QK_PAD:(h + 1) * QK_PAD, :], cqn)
        qbt = _dot_nt(wqbt_ref[h * LANES:(h + 1) * LANES, :], cqn)
        qt_out[h, 0:LANES, :] = (qat[0:LANES, :] * qscale).astype(BF16)
        qt_out[h, LANES:QK_PAD, :] = ((qat[LANES:QK_PAD, :] * t1t + qbt * t2t) * qscale).astype(BF16)
        k_out[h, :, 0:LANES] = _dot(ckvn, wk_ref[:, h * LANES:(h + 1) * LANES]).astype(BF16)
        k_out[h, :, LANES:QK_PAD] = krot
        vt = _dot_nt(wvt_ref[h * D_V:(h + 1) * D_V, :], ckvn).astype(BF16)
        for bk in range(tm // vb):
            vt_out[h, bk] = vt[:, bk * vb:(bk + 1) * vb]

    bg = _dot(hb, win_ref[:, _OFF_BG:_OFF_BG + conv_dim])
    cg = _dot(hb, win_ref[:, off_cg:off_cg + conv_dim])
    hc = _dot(hb, win_ref[:, off_hc:off_hc + conv_dim])
    u = cg * hc
    ubuf[8:8 + tm, :] = u
    u1 = ubuf[7:7 + tm, :]
    u2 = ubuf[6:6 + tm, :]
    w = convw_ref[...]
    y = bg * (u2 * w[0:1, :] + u1 * w[1:2, :] + u * w[2:3, :])
    convn_out[...] = _rms_norm(y, convg_ref[...]).astype(BF16)
    tail = ubuf[tm:tm + 8, :]
    utail_out[...] = tail
    ubuf[0:8, :] = tail


def _inproj(x, t1, t2, t1t, t2t, uinit, ln0g, ln0b, win, qg, kvg, wqat, wqbt, wk, wvt, convw, convg, *, tm):
    b, s, d = x.shape
    conv_dim = convw.shape[1]
    grid = (b, s // tm)
    h = MLA_HEADS
    vb = min(tm, _KV_BLOCK)
    out_shape = (
        jax.ShapeDtypeStruct((b, h, QK_PAD, s), BF16),
        jax.ShapeDtypeStruct((b, h, s, QK_PAD), BF16),
        jax.ShapeDtypeStruct((b, h, s // vb, D_V, vb), BF16),
        jax.ShapeDtypeStruct((b, s, conv_dim), BF16),
        jax.ShapeDtypeStruct((b, 8, conv_dim), F32),
    )
    in_specs = [
        pl.BlockSpec((None, tm, d), lambda bi, i: (bi, i, 0)),
        pl.BlockSpec((tm, LANES), lambda bi, i: (i, 0)),
        pl.BlockSpec((tm, LANES), lambda bi, i: (i, 0)),
        pl.BlockSpec((LANES, tm), lambda bi, i: (0, i)),
        pl.BlockSpec((LANES, tm), lambda bi, i: (0, i)),
        _const_spec(uinit.shape), _const_spec(ln0g.shape), _const_spec(ln0b.shape), _const_spec(win.shape),
        _const_spec(qg.shape), _const_spec(kvg.shape), _const_spec(wqat.shape), _const_spec(wqbt.shape),
        _const_spec(wk.shape), _const_spec(wvt.shape), _const_spec(convw.shape), _const_spec(convg.shape),
    ]
    out_specs = (
        pl.BlockSpec((None, h, QK_PAD, tm), lambda bi, i: (bi, 0, 0, i)),
        pl.BlockSpec((None, h, tm, QK_PAD), lambda bi, i: (bi, 0, i, 0)),
        pl.BlockSpec((None, h, tm // vb, D_V, vb), lambda bi, i: (bi, 0, i, 0, 0)),
        pl.BlockSpec((None, tm, conv_dim), lambda bi, i: (bi, i, 0)),
        pl.BlockSpec((None, 8, conv_dim), lambda bi, i: (bi, 0, 0)),
    )
    return pl.pallas_call(
        functools.partial(_inproj_kernel, tm=tm, vb=vb, conv_dim=conv_dim),
        grid=grid, in_specs=in_specs, out_specs=out_specs, out_shape=out_shape,
        scratch_shapes=[pltpu.VMEM((tm + 8, conv_dim), F32)],
        compiler_params=pltpu.CompilerParams(
            dimension_semantics=("arbitrary", "arbitrary"), vmem_limit_bytes=VMEM_LIMIT),
        name="inproj",
    )(x, t1, t2, t1t, t2t, uinit, ln0g, ln0b, win, qg, kvg, wqat, wqbt, wk, wvt, convw, convg)


def _attn_kernel(qt_ref, k_ref, vt_ref, km_ref, vmt_ref, o_ref, acc_ref, sa_ref, sb_ref, *, tq):
    i = pl.program_id(2)
    kb, qb = _KV_BLOCK, _Q_BLOCK
    ncb = tq // qb
    nd = tq // kb
    assert nd == 2, "the two-buffer pipeline consumes key blocks in pairs"

    def cols(cb):
        return slice(cb * qb, (cb + 1) * qb)

    def scores(j, first_cb=0):
        k = k_ref[pl.ds(pl.multiple_of(j * kb, kb), kb), :]
        return _dot(k, qt_ref[:, first_cb * qb:])

    def absorb(cb, m, l, s, smax, vt):
        mn = jnp.maximum(m, smax)
        a = jnp.exp2(m - mn)
        p = jnp.exp2(s - mn)
        l = a * l + jnp.sum(p, axis=0, keepdims=True)
        acc_ref[:, cols(cb)] = a * acc_ref[:, cols(cb)] + _dot(vt, p.astype(BF16))
        return mn, l

    def absorb_all(carry, s_ref, smax, vt):
        ms, ls = carry
        out = [absorb(cb, ms[cb], ls[cb], s_ref[:, cols(cb)], smax[cb], vt) for cb in range(ncb)]
        return tuple(o[0] for o in out), tuple(o[1] for o in out)

    def stash(s_ref, j):
        s = scores(j)
        s_ref[...] = s
        return tuple(jnp.max(s[:, cols(cb)], axis=0, keepdims=True) for cb in range(ncb))

    max_a = stash(sa_ref, 0)

    ms, ls = [], []
    meta_rows = lax.broadcasted_iota(jnp.int32, (km_ref.shape[0], qb), 0) < N_META
    for cb in range(ncb):
        s = jnp.where(meta_rows, _dot(km_ref[...], qt_ref[:, cols(cb)]), NEG_INF)
        m0 = jnp.max(s, axis=0, keepdims=True)
        p0 = jnp.exp2(s - m0)
        ms.append(m0)
        ls.append(jnp.sum(p0, axis=0, keepdims=True))
        acc_ref[:, cols(cb)] = _dot(vmt_ref[...], p0.astype(BF16))

    def body(jp, carry):
        stats, max_a = carry
        j = 2 * jp
        max_b = stash(sb_ref, j + 1)
        stats = absorb_all(stats, sa_ref, max_a, vt_ref[j])
        max_a = stash(sa_ref, j + 2)
        return absorb_all(stats, sb_ref, max_b, vt_ref[j + 1]), max_a

    (ms, ls), _ = lax.fori_loop(0, i, body, ((tuple(ms), tuple(ls)), max_a))
    ms, ls = list(ms), list(ls)

    for jj in range(nd):
        j = i * nd + jj
        first_cb = (jj * kb) // qb
        s_all = None if jj == 0 else scores(j, first_cb)
        vt = vt_ref[j]
        for cb in range(first_cb, ncb):
            s = sa_ref[:, cols(cb)] if jj == 0 else s_all[:, (cb - first_cb) * qb:(cb - first_cb + 1) * qb]
            if (jj + 1) * kb > cb * qb + CHUNK:
                key_chunk = (lax.broadcasted_iota(jnp.int32, (kb, qb), 0) + jj * kb) >> _CHUNK_SHIFT
                qry_chunk = (lax.broadcasted_iota(jnp.int32, (kb, qb), 1) + cb * qb) >> _CHUNK_SHIFT
                s = jnp.where(key_chunk <= qry_chunk, s, NEG_INF)
            ms[cb], ls[cb] = absorb(cb, ms[cb], ls[cb], s, jnp.max(s, axis=0, keepdims=True), vt)

    for cb in range(ncb):
        o_ref[cb * qb:(cb + 1) * qb, :] = (acc_ref[:, cols(cb)] / ls[cb]).T.astype(BF16)


def _attention(qt, k, vt, km, vmt, *, tq):
    b, h, _, s = qt.shape
    grid = (b, h, s // tq)
    nkb = vt.shape[2]
    return pl.pallas_call(
        functools.partial(_attn_kernel, tq=tq),
        grid=grid,
        in_specs=[
            pl.BlockSpec((None, None, QK_PAD, tq), lambda bi, hi, i: (bi, hi, 0, i)),
            pl.BlockSpec((None, None, s, QK_PAD), lambda bi, hi, i: (bi, hi, 0, 0)),
            pl.BlockSpec((None, None, nkb, D_V, _KV_BLOCK), lambda bi, hi, i: (bi, hi, 0, 0, 0)),
            pl.BlockSpec((None, LANES, QK_PAD), lambda bi, hi, i: (hi, 0, 0)),
            pl.BlockSpec((None, D_V, LANES), lambda bi, hi, i: (hi, 0, 0)),
        ],
        out_specs=pl.BlockSpec((None, tq, D_V), lambda bi, hi, i: (bi, i, hi)),
        out_shape=jax.ShapeDtypeStruct((b, s, h * D_V), BF16),
        scratch_shapes=[pltpu.VMEM((D_V, tq), F32), pltpu.VMEM((_KV_BLOCK, tq), F32),
                        pltpu.VMEM((_KV_BLOCK, tq), F32)],
        compiler_params=pltpu.CompilerParams(
            dimension_semantics=("arbitrary", "arbitrary", "arbitrary"), vmem_limit_bytes=VMEM_LIMIT),
        name="attn",
    )(qt, k, vt, km, vmt)


def _outproj_kernel(x_ref, oa_ref, cn_ref, ln0g_ref, ln0b_ref, ag_ref, wo_ref, ln1g_ref, ln1b_ref,
                    wq_ref, sk_ref, h1_out, h1b_out, st_out):
    h0 = _layer_norm(x_ref[...], ln0g_ref[...], ln0b_ref[...])
    an = _rms_norm(oa_ref[...].astype(F32), ag_ref[...]).astype(BF16)
    mixed = _dot(an, wo_ref[0:ATTN_DIM, :]) + _dot(cn_ref[...], wo_ref[ATTN_DIM:, :])
    h1 = _layer_norm(DEEPNORM_ALPHA * h0 + mixed, ln1g_ref[...], ln1b_ref[...])
    h1_out[...] = h1
    h1b = h1.astype(BF16)
    h1b_out[...] = h1b
    pq = _dot(h1b, wq_ref[...])
    for hc in range(PEER_HEADS * 2):
        pqs = pq[:, hc * D_HALF:(hc + 1) * D_HALF].astype(BF16)
        st_out[hc * N_KEYS:(hc + 1) * N_KEYS, :] = _dot_nt(sk_ref[hc % 2], pqs)


def _outproj(x, oa, cn, ln0g, ln0b, ag, wo, ln1g, ln1b, wq, sk, *, tm):
    b, s, d = x.shape
    grid = (b, s // tm)
    nsc = PEER_HEADS * 2 * N_KEYS
    tok = lambda w: pl.BlockSpec((None, tm, w), lambda bi, i: (bi, i, 0))
    return pl.pallas_call(
        _outproj_kernel,
        grid=grid,
        in_specs=[tok(d), tok(oa.shape[2]), tok(cn.shape[2]),
                  _const_spec(ln0g.shape), _const_spec(ln0b.shape), _const_spec(ag.shape), _const_spec(wo.shape),
                  _const_spec(ln1g.shape), _const_spec(ln1b.shape), _const_spec(wq.shape), _const_spec(sk.shape)],
        out_specs=(tok(d), tok(d), pl.BlockSpec((None, nsc, tm), lambda bi, i: (bi, 0, i))),
        out_shape=(jax.ShapeDtypeStruct((b, s, d), F32), jax.ShapeDtypeStruct((b, s, d), BF16),
                   jax.ShapeDtypeStruct((b, nsc, s), F32)),
        compiler_params=pltpu.CompilerParams(
            dimension_semantics=("arbitrary", "arbitrary"), vmem_limit_bytes=VMEM_LIMIT),
        name="outproj",
    )(x, oa, cn, ln0g, ln0b, ag, wo, ln1g, ln1b, wq, sk)


_BIG_ID = 1 << 20


def _extract_top(s, ids, n):
    w = s.shape[1]
    rows = lax.broadcasted_iota(jnp.int32, (n, w), 0)
    vals = jnp.zeros((n, w), F32)
    sel = jnp.zeros((n, w), jnp.int32)
    val_rows = []
    for r in range(n):
        m = jnp.max(s, axis=0, keepdims=True)
        pick = jnp.min(jnp.where(s == m, ids, _BIG_ID), axis=0, keepdims=True)
        vals = jnp.where(rows == r, m, vals)
        sel = jnp.where(rows == r, pick, sel)
        val_rows.append(m)
        s = jnp.where(ids == pick, -jnp.inf, s)
    return vals, sel, val_rows


_SUBLANES = 8


def _top_keys(s_ref, base, n):
    w = s_ref.shape[1]
    nv = N_KEYS // _SUBLANES
    assert n <= nv
    sub = lax.broadcasted_iota(jnp.int32, (_SUBLANES, w), 0)
    vals = [s_ref[pl.ds(pl.multiple_of(base + r * _SUBLANES, _SUBLANES), _SUBLANES), :] for r in range(nv)]
    ids = [sub + r * _SUBLANES for r in range(nv)]
    for rnd in range(nv):
        for i in range(rnd % 2, nv - 1, 2):
            swap = vals[i + 1] > vals[i]
            hi, lo = jnp.maximum(vals[i], vals[i + 1]), jnp.minimum(vals[i], vals[i + 1])
            ids[i], ids[i + 1] = jnp.where(swap, ids[i + 1], ids[i]), jnp.where(swap, ids[i], ids[i + 1])
            vals[i], vals[i + 1] = hi, lo
    rows = lax.broadcasted_iota(jnp.int32, (n, w), 0)
    out_v = jnp.zeros((n, w), F32)
    out_i = jnp.zeros((n, w), jnp.int32)
    val_rows = []
    for t in range(n):
        m = jnp.max(vals[0], axis=0, keepdims=True)
        pick = jnp.min(jnp.where(vals[0] == m, ids[0], _BIG_ID), axis=0, keepdims=True)
        out_v = jnp.where(rows == t, m, out_v)
        out_i = jnp.where(rows == t, pick, out_i)
        val_rows.append(m)
        win = ids[0] == pick
        for r in range(n - 1 - t):
            vals[r] = jnp.where(win, vals[r + 1], vals[r])
            ids[r] = jnp.where(win, ids[r + 1], ids[r])
    return out_v, out_i, val_rows


def _route_kernel(st_ref, i1_out, i2_out, g_out):
    w = st_ref.shape[1]
    k = PEER_TOPK
    r16 = lax.broadcasted_iota(jnp.int32, (k, w), 0)
    r8 = lax.broadcasted_iota(jnp.int32, (8, w), 0)

    def head(h, _):
        base = pl.multiple_of(h * 2 * N_KEYS, 2 * N_KEYS)
        v1, i1, v1rows = _top_keys(st_ref, base, k)
        v2, i2, v2rows = _top_keys(st_ref, base + N_KEYS, k)
        pieces, piece_ids = [], []

        def add(vals, ids, ok):
            pieces.append(jnp.where(ok, vals, -jnp.inf))
            piece_ids.append(ids)

        add(v1rows[0] + v2, r16, r16 < k)
        for a in (1, 2, 3):
            add(v1rows[a] + v2[0:8, :], a * k + r8, (a + 1) * (r8 + 1) <= k)
        add(v1 + v2rows[0], r16 * k, r16 >= 4)
        for bb in (1, 2):
            add(v1[0:8, :] + v2rows[bb], r8 * k + bb, (r8 >= 4) & ((r8 + 1) * (bb + 1) <= k))
        cand = jnp.concatenate(pieces, axis=0)
        cand_ids = jnp.concatenate(piece_ids, axis=0)
        best, best_id, _ = _extract_top(cand, cand_ids, k)
        a_sel = best_id >> _TOPK_SHIFT
        b_sel = best_id & (k - 1)
        e1 = jnp.zeros((k, w), jnp.int32)
        e2 = jnp.zeros((k, w), jnp.int32)
        for c in range(k):
            e1 = jnp.where(a_sel == c, i1[c:c + 1, :], e1)
            e2 = jnp.where(b_sel == c, i2[c:c + 1, :], e2)
        ex = jnp.exp(best - best[0:1, :])
        gate = ex / jnp.sum(ex, axis=0, keepdims=True)
        o = pl.multiple_of(h * k, k)
        i1_out[pl.ds(o, k), :] = e1
        i2_out[pl.ds(o, k), :] = e2
        g_out[pl.ds(o, k), :] = gate
        return 0

    lax.fori_loop(0, PEER_HEADS, head, 0, unroll=2)


def _route(st, *, tl):
    b, nsc, s = st.shape
    nj = PEER_HEADS * PEER_TOPK
    grid = (b, s // tl)
    ospec = pl.BlockSpec((None, nj, tl), lambda bi, i: (bi, 0, i))
    return pl.pallas_call(
        _route_kernel,
        grid=grid,
        in_specs=[pl.BlockSpec((None, nsc, tl), lambda bi, i: (bi, 0, i))],
        out_specs=(ospec, ospec, ospec),
        out_shape=(jax.ShapeDtypeStruct((b, nj, s), jnp.int32), jax.ShapeDtypeStruct((b, nj, s), jnp.int32),
                   jax.ShapeDtypeStruct((b, nj, s), F32)),
        compiler_params=pltpu.CompilerParams(dimension_semantics=("arbitrary", "arbitrary")),
        name="route",
    )(st)


_GROUP = 16
_PITCH = N_KEYS + 4


def _scatter_kernel(i1_ref, i2_ref, g_ref, gd_out, stage_a, stage_b):
    tg = i1_ref.shape[0]
    nj = i1_ref.shape[1]
    sub = lax.broadcasted_iota(jnp.int32, (N_KEYS, nj), 0)
    stages = (stage_a, stage_b)

    def fill(grp):
        stage = stages[grp % 2]
        for t in range(_GROUP):
            tok = grp * _GROUP + t
            r1 = i1_ref[tok:tok + 1, :]
            r2 = i2_ref[tok:tok + 1, :]
            gg = g_ref[tok:tok + 1, :]
            p1 = jnp.where(sub == r1, gg, 0.0).astype(BF16)
            p2 = jnp.where(sub == r2, 1.0, 0.0).astype(BF16)
            stage[t * _PITCH:t * _PITCH + N_KEYS, :] = _dot_nt(p1, p2)

    def drain(grp):
        stage = stages[grp % 2]
        for a in range(N_KEYS):
            rows = stage[pl.ds(a, _GROUP, stride=_PITCH), :]
            gd_out[grp * _GROUP:(grp + 1) * _GROUP, a * N_KEYS:(a + 1) * N_KEYS] = rows.astype(BF16)

    ngroups = tg // _GROUP
    fill(0)
    for grp in range(ngroups):
        if grp + 1 < ngroups:
            fill(grp + 1)
        drain(grp)


def _scatter(i1, i2, g, *, tg):
    t, nj = i1.shape
    ne = N_KEYS * N_KEYS
    spec = pl.BlockSpec((tg, nj), lambda i: (i, 0))
    return pl.pallas_call(
        _scatter_kernel,
        grid=(t // tg,),
        in_specs=[spec, spec, spec],
        out_specs=pl.BlockSpec((tg, ne), lambda i: (i, 0)),
        out_shape=jax.ShapeDtypeStruct((t, ne), BF16),
        scratch_shapes=[pltpu.VMEM((_GROUP * _PITCH, N_KEYS), F32), pltpu.VMEM((_GROUP * _PITCH, N_KEYS), F32)],
        compiler_params=pltpu.CompilerParams(dimension_semantics=("arbitrary",)),
        name="scatter",
    )(i1, i2, g)


def _peer_kernel(xb_ref, h1_ref, ut_ref, v_ref, gd_ref, ln2g_ref, ln2b_ref, o_ref, acc):
    e = pl.program_id(1)

    @pl.when(e == 0)
    def _():
        acc[...] = jnp.zeros_like(acc)

    a = _dot(xb_ref[...], ut_ref[...])
    act = 0.5 * a * (1.0 + lax.erf(a * (2.0 ** -0.5)))
    hd = (act * gd_ref[...].astype(F32)).astype(BF16)
    acc[...] += _dot(hd, v_ref[...])

    @pl.when(e == pl.num_programs(1) - 1)
    def _():
        o_ref[...] = _layer_norm(DEEPNORM_ALPHA * h1_ref[...] + acc[...], ln2g_ref[...], ln2b_ref[...])


def _peer(xb, h1, ut, v, gd, ln2g, ln2b, *, tm, ec):
    t, d = xb.shape
    ne = v.shape[0]
    grid = (t // tm, ne // ec)
    return pl.pallas_call(
        _peer_kernel,
        grid=grid,
        in_specs=[
            pl.BlockSpec((tm, d), lambda i, e: (i, 0)),
            pl.BlockSpec((tm, d), lambda i, e: (i, 0)),
            pl.BlockSpec((d, ec), lambda i, e: (0, e)),
            pl.BlockSpec((ec, d), lambda i, e: (e, 0)),
            pl.BlockSpec((tm, ec), lambda i, e: (i, e)),
            _const_spec(ln2g.shape), _const_spec(ln2b.shape),
        ],
        out_specs=pl.BlockSpec((tm, d), lambda i, e: (i, 0)),
        out_shape=jax.ShapeDtypeStruct((t, d), F32),
        scratch_shapes=[pltpu.VMEM((tm, d), F32)],
        compiler_params=pltpu.CompilerParams(
            dimension_semantics=("arbitrary", "arbitrary"), vmem_limit_bytes=VMEM_LIMIT),
        name="peer",
    )(xb, h1, ut, v, gd, ln2g, ln2b)


def _rope_tables(first_pos, n):
    inv = 1.0 / (ROPE_THETA ** (jnp.arange(0, D_ROPE, 2, dtype=F32) / D_ROPE))
    ang = (first_pos + jnp.arange(n, dtype=F32))[:, None] * inv[None, :]
    cos, sin = jnp.cos(ang), jnp.sin(ang)
    z = jnp.zeros((n, LANES - D_ROPE), F32)
    return jnp.concatenate([cos, cos, z], axis=1), jnp.concatenate([-sin, sin, z], axis=1)


def _swap_halves(w):
    half = w.shape[-1] // 2
    return jnp.concatenate([w[..., half:], w[..., :half]], axis=-1)


def _pick(n, prefs):
    for p in prefs:
        if n % p == 0:
            return p
    raise ValueError(f"no tile in {prefs} divides {n}")


def kernel(x, meta_tokens, ln0_g, ln0_b, w_in, q_norm_g, kv_norm_g, w_uq, w_ukv, conv_w, attn_out_g, conv_out_g,
           w_o, ln1_g, ln1_b, peer_w_query, peer_sub_keys, peer_u, peer_v, ln2_g, ln2_b):
    b, s, d = x.shape
    assert w_in.shape[0] == 1 and s % _Q_TILE == 0
    conv_dim = conv_w.shape[2]
    h = MLA_HEADS
    row = lambda a: a.reshape(1, -1).astype(F32)

    wi = w_in[0]
    o_kv, o_kr, o_b = Q_RANK, Q_RANK + KV_RANK, Q_RANK + KV_RANK + D_ROPE
    w_kr = wi[:, o_kr:o_b]
    zpad = jnp.zeros((d, LANES - D_ROPE), F32)
    win = jnp.concatenate(
        [wi[:, :o_kr], w_kr, zpad, _swap_halves(w_kr), zpad, wi[:, o_b:]], axis=1).astype(BF16)
    wq3 = w_uq[0].reshape(Q_RANK, h, D_QK)
    zq = jnp.zeros((Q_RANK, h, QK_PAD - D_QK), F32)
    wqat = jnp.concatenate([wq3, zq], axis=2).reshape(Q_RANK, h * QK_PAD).T.astype(BF16)
    wqbt = jnp.concatenate([_swap_halves(wq3[:, :, D_NOPE:]), zq], axis=2).reshape(Q_RANK, h * LANES).T.astype(BF16)
    wkv3 = w_ukv[0].reshape(KV_RANK, h, D_NOPE + D_V)
    wk = wkv3[:, :, :D_NOPE].reshape(KV_RANK, h * D_NOPE).astype(BF16)
    wvt = wkv3[:, :, D_NOPE:].reshape(KV_RANK, h * D_V).T.astype(BF16)
    wo = w_o[0].astype(BF16)
    wpq = peer_w_query[0].astype(BF16)
    sk = peer_sub_keys[0].astype(BF16)
    ut = peer_u[0].T.astype(BF16)
    pv = peer_v[0].astype(BF16)

    inproj_w = (row(ln0_g), row(ln0_b), win, row(q_norm_g[0]), row(kv_norm_g[0]), wqat, wqbt, wk, wvt,
                conv_w[0].astype(F32), row(conv_out_g[0]))

    t1m, t2m = _rope_tables(0, N_META)
    _, km, vmt, _, utail = _inproj(meta_tokens[None].astype(F32), t1m, t2m, t1m.T, t2m.T,
                                   jnp.zeros((8, conv_dim), F32), *inproj_w, tm=N_META)
    km = jnp.pad(km[0], ((0, 0), (0, LANES - N_META), (0, 0)))
    vmt = jnp.pad(vmt[0, :, 0], ((0, 0), (0, 0), (0, LANES - N_META)))

    t1, t2 = _rope_tables(N_META, s)
    qt, k, vt, convn, _ = _inproj(x, t1, t2, t1.T, t2.T, utail[0], *inproj_w, tm=_KV_BLOCK)
    oa = _attention(qt, k, vt, km, vmt, tq=_Q_TILE)
    h1, h1b, st = _outproj(x, oa, convn, row(ln0_g), row(ln0_b), row(attn_out_g[0]), wo, row(ln1_g[0]),
                           row(ln1_b[0]), wpq, sk, tm=_pick(s, (256, 128)))
    i1, i2, g = _route(st, tl=LANES)
    tok = lambda a: jnp.swapaxes(a, 1, 2).reshape(b * s, a.shape[1])
    gd = _scatter(tok(i1), tok(i2), tok(g), tg=_pick(b * s, (128, 64)))
    out = _peer(h1b.reshape(b * s, d), h1.reshape(b * s, d), ut, pv, gd, row(ln2_g[0]), row(ln2_b[0]),
                tm=_pick(b * s, (512, 256, 128)), ec=1024)
    return out.reshape(b, s, d)
```

```python
import functools

import jax
import jax.numpy as jnp
import numpy as np
from jax import lax
from jax.experimental import pallas as pl
from jax.experimental.pallas import tpu as pltpu

CHUNK = 64
N_META = 16
MLA_HEADS = 8
D_NOPE = 128
D_ROPE = 64
D_QK = D_NOPE + D_ROPE
D_V = 128
Q_RANK = 384
KV_RANK = 512
ROPE_THETA = 10000.0
ATTN_DIM = MLA_HEADS * D_V
CONV_WIDTH = 3
PEER_HEADS = 8
N_KEYS = 128
D_HALF = 128
PEER_TOPK = 16
DEPTH = 1
DEEPNORM_ALPHA = (2.0 * DEPTH) ** 0.25
EPS = 1e-5
NEG_INF = -1e30

_CHUNK_SHIFT = CHUNK.bit_length() - 1
_TOPK_SHIFT = PEER_TOPK.bit_length() - 1
assert 1 << _CHUNK_SHIFT == CHUNK and 1 << _TOPK_SHIFT == PEER_TOPK

LANES = 128
QK_PAD = 256
_KV_BLOCK = 512
_Q_BLOCK = 256
_Q_TILE = 2 * _KV_BLOCK
_DENOM_ROWS = 16
_LOG2E = 1.4426950408889634
VMEM_LIMIT = 58 * 1024 * 1024

F32 = jnp.float32
BF16 = jnp.bfloat16


def _dot(a, b):
    return jnp.dot(a, b, preferred_element_type=F32)


def _dot_nt(a, b):
    return lax.dot_general(a, b, (((1,), (1,)), ((), ())), preferred_element_type=F32)


def _layer_norm(x, g, b):
    mu = jnp.mean(x, axis=-1, keepdims=True)
    xc = x - mu
    var = jnp.mean(xc * xc, axis=-1, keepdims=True)
    return xc * lax.rsqrt(var + EPS) * g + b


def _rms_norm(x, g):
    return x * lax.rsqrt(jnp.mean(x * x, axis=-1, keepdims=True) + EPS) * g


def _const_spec(shape):
    nd = len(shape)
    return pl.BlockSpec(shape, lambda *_: (0,) * nd, pipeline_mode=pl.Buffered(1))


_OFF_CQ = 0
_OFF_CKV = _OFF_CQ + Q_RANK
_OFF_KRA = _OFF_CKV + KV_RANK
_OFF_KRB = _OFF_KRA + LANES
_OFF_BG = _OFF_KRB + LANES


def _inproj_kernel(x_ref, t1_ref, t2_ref, t1t_ref, t2t_ref, uinit_ref, ln0g_ref, ln0b_ref, win_ref, qg_ref,
                   kvg_ref, wqat_ref, wqbt_ref, wk_ref, wvt_ref, convw_ref, convg_ref,
                   qt_out, k_out, vt_out, convn_out, utail_out, ubuf, *, tm, vb, conv_dim):
    i = pl.program_id(1)
    off_cg = _OFF_BG + conv_dim
    off_hc = off_cg + conv_dim

    @pl.when(i == 0)
    def _():
        ubuf[0:8, :] = uinit_ref[...]

    hb = _layer_norm(x_ref[...], ln0g_ref[...], ln0b_ref[...]).astype(BF16)
    t1 = t1_ref[...]
    t2 = t2_ref[...]
    t1t = t1t_ref[...]
    t2t = t2t_ref[...]
    qscale = D_QK ** -0.5 * _LOG2E

    cq = _dot(hb, win_ref[:, _OFF_CQ:_OFF_CQ + Q_RANK])
    ckv = _dot(hb, win_ref[:, _OFF_CKV:_OFF_CKV + KV_RANK])
    kra = _dot(hb, win_ref[:, _OFF_KRA:_OFF_KRA + LANES])
    krb = _dot(hb, win_ref[:, _OFF_KRB:_OFF_KRB + LANES])
    krot = (kra * t1 + krb * t2).astype(BF16)
    cqn = _rms_norm(cq, qg_ref[...]).astype(BF16)
    ckvn = _rms_norm(ckv, kvg_ref[...]).astype(BF16)

    for h in range(MLA_HEADS):
        qat = _dot_nt(wqat_ref[h * QK_PAD:(h + 1) * QK_PAD, :], cqn)
        qbt = _dot_nt(wqbt_ref[h * LANES:(h + 1) * LANES, :], cqn)
        qt_out[h, 0:LANES, :] = (qat[0:LANES, :] * qscale).astype(BF16)
        qt_out[h, LANES:QK_PAD, :] = ((qat[LANES:QK_PAD, :] * t1t + qbt * t2t) * qscale).astype(BF16)
        k_out[h, :, 0:LANES] = _dot(ckvn, wk_ref[:, h * LANES:(h + 1) * LANES]).astype(BF16)
        k_out[h, :, LANES:QK_PAD] = krot
        vt = _dot_nt(wvt_ref[h * D_V:(h + 1) * D_V, :], ckvn).astype(BF16)
        for bk in range(tm // vb):
            vt_out[h, bk] = vt[:, bk * vb:(bk + 1) * vb]

    bg = _dot(hb, win_ref[:, _OFF_BG:_OFF_BG + conv_dim])
    cg = _dot(hb, win_ref[:, off_cg:off_cg + conv_dim])
    hc = _dot(hb, win_ref[:, off_hc:off_hc + conv_dim])
    u = cg * hc
    ubuf[8:8 + tm, :] = u
    u1 = ubuf[7:7 + tm, :]
    u2 = ubuf[6:6 + tm, :]
    w = convw_ref[...]
    y = bg * (u2 * w[0:1, :] + u1 * w[1:2, :] + u * w[2:3, :])
    convn_out[...] = _rms_norm(y, convg_ref[...]).astype(BF16)
    tail = ubuf[tm:tm + 8, :]
    utail_out[...] = tail
    ubuf[0:8, :] = tail


def _inproj(x, t1, t2, t1t, t2t, uinit, ln0g, ln0b, win, qg, kvg, wqat, wqbt, wk, wvt, convw, convg, *, tm):
    b, s, d = x.shape
    conv_dim = convw.shape[1]
    grid = (b, s // tm)
    h = MLA_HEADS
    vb = min(tm, _KV_BLOCK)
    out_shape = (
        jax.ShapeDtypeStruct((b, h, QK_PAD, s), BF16),
        jax.ShapeDtypeStruct((b, h, s, QK_PAD), BF16),
        jax.ShapeDtypeStruct((b, h, s // vb, D_V, vb), BF16),
        jax.ShapeDtypeStruct((b, s, conv_dim), BF16),
        jax.ShapeDtypeStruct((b, 8, conv_dim), F32),
    )
    in_specs = [
        pl.BlockSpec((None, tm, d), lambda bi, i: (bi, i, 0)),
        pl.BlockSpec((tm, LANES), lambda bi, i: (i, 0)),
        pl.BlockSpec((tm, LANES), lambda bi, i: (i, 0)),
        pl.BlockSpec((LANES, tm), lambda bi, i: (0, i)),
        pl.BlockSpec((LANES, tm), lambda bi, i: (0, i)),
        _const_spec(uinit.shape), _const_spec(ln0g.shape), _const_spec(ln0b.shape), _const_spec(win.shape),
        _const_spec(qg.shape), _const_spec(kvg.shape), _const_spec(wqat.shape), _const_spec(wqbt.shape),
        _const_spec(wk.shape), _const_spec(wvt.shape), _const_spec(convw.shape), _const_spec(convg.shape),
    ]
    out_specs = (
        pl.BlockSpec((None, h, QK_PAD, tm), lambda bi, i: (bi, 0, 0, i)),
        pl.BlockSpec((None, h, tm, QK_PAD), lambda bi, i: (bi, 0, i, 0)),
        pl.BlockSpec((None, h, tm // vb, D_V, vb), lambda bi, i: (bi, 0, i, 0, 0)),
        pl.BlockSpec((None, tm, conv_dim), lambda bi, i: (bi, i, 0)),
        pl.BlockSpec((None, 8, conv_dim), lambda bi, i: (bi, 0, 0)),
    )
    return pl.pallas_call(
        functools.partial(_inproj_kernel, tm=tm, vb=vb, conv_dim=conv_dim),
        grid=grid, in_specs=in_specs, out_specs=out_specs, out_shape=out_shape,
        scratch_shapes=[pltpu.VMEM((tm + 8, conv_dim), F32)],
        compiler_params=pltpu.CompilerParams(
            dimension_semantics=("arbitrary", "arbitrary"), vmem_limit_bytes=VMEM_LIMIT),
        name="inproj",
    )(x, t1, t2, t1t, t2t, uinit, ln0g, ln0b, win, qg, kvg, wqat, wqbt, wk, wvt, convw, convg)


def _attn_kernel(qt_ref, k_ref, vt_ref, km_ref, vmt_ref, o_ref, acc_ref, sa_ref, sb_ref, *, tq):
    i = pl.program_id(2)
    kb, qb = _KV_BLOCK, _Q_BLOCK
    ncb = tq // qb
    nd = tq // kb
    assert nd == 2, "the two-buffer pipeline consumes key blocks in pairs"

    def cols(cb):
        return slice(cb * qb, (cb + 1) * qb)

    def scores(j, first_cb=0):
        k = k_ref[pl.ds(pl.multiple_of(j * kb, kb), kb), :]
        return _dot(k, qt_ref[:, first_cb * qb:])

    def with_ones(vt):
        return jnp.concatenate([vt, jnp.ones((_DENOM_ROWS, vt.shape[1]), BF16)], axis=0)

    def absorb(cb, m, s, smax, vt1):
        mn = jnp.maximum(m, smax)
        a = jnp.exp2(m - mn)
        p = jnp.exp2(s - mn)
        acc_ref[:, cols(cb)] = a * acc_ref[:, cols(cb)] + _dot(vt1, p.astype(BF16))
        return mn

    def stash(s_ref, k, cb):
        s = _dot(k, qt_ref[:, cols(cb)])
        s_ref[:, cols(cb)] = s
        return jnp.max(s, axis=0, keepdims=True)

    def key_block(j):
        return k_ref[pl.ds(pl.multiple_of(j * kb, kb), kb), :]

    def trade(carry, s_old, smax_old, vt_old, s_new, j_new):
        k_new = key_block(j_new)
        vt1 = with_ones(vt_old)
        ms, smax_new = list(carry), []
        for cb in range(ncb):
            s = s_old[:, cols(cb)]
            smax_new.append(stash(s_new, k_new, cb))
            ms[cb] = absorb(cb, ms[cb], s, smax_old[cb], vt1)
        return tuple(ms), tuple(smax_new)

    k0 = key_block(0)
    max_a = tuple(stash(sa_ref, k0, cb) for cb in range(ncb))

    ms = []
    meta_rows = lax.broadcasted_iota(jnp.int32, (km_ref.shape[0], qb), 0) < N_META
    vmt1 = with_ones(vmt_ref[...])
    for cb in range(ncb):
        s = jnp.where(meta_rows, _dot(km_ref[...], qt_ref[:, cols(cb)]), NEG_INF)
        m0 = jnp.max(s, axis=0, keepdims=True)
        ms.append(m0)
        acc_ref[:, cols(cb)] = _dot(vmt1, jnp.exp2(s - m0).astype(BF16))

    def body(jp, carry):
        stats, max_a = carry
        j = 2 * jp
        stats, max_b = trade(stats, sa_ref, max_a, vt_ref[j], sb_ref, j + 1)
        return trade(stats, sb_ref, max_b, vt_ref[j + 1], sa_ref, j + 2)

    ms, _ = lax.fori_loop(0, i, body, (tuple(ms), max_a))
    ms = list(ms)

    for jj in range(nd):
        j = i * nd + jj
        first_cb = (jj * kb) // qb
        s_all = None if jj == 0 else scores(j, first_cb)
        vt1 = with_ones(vt_ref[j])
        for cb in range(first_cb, ncb):
            s = sa_ref[:, cols(cb)] if jj == 0 else s_all[:, (cb - first_cb) * qb:(cb - first_cb + 1) * qb]
            if (jj + 1) * kb > cb * qb + CHUNK:
                key_chunk = (lax.broadcasted_iota(jnp.int32, (kb, qb), 0) + jj * kb) >> _CHUNK_SHIFT
                qry_chunk = (lax.broadcasted_iota(jnp.int32, (kb, qb), 1) + cb * qb) >> _CHUNK_SHIFT
                s = jnp.where(key_chunk <= qry_chunk, s, NEG_INF)
            ms[cb] = absorb(cb, ms[cb], s, jnp.max(s, axis=0, keepdims=True), vt1)

    for cb in range(ncb):
        o = acc_ref[0:D_V, cols(cb)] / acc_ref[D_V:D_V + 1, cols(cb)]
        o_ref[cb * qb:(cb + 1) * qb, :] = o.T.astype(BF16)


def _attention(qt, k, vt, km, vmt, *, tq):
    b, h, _, s = qt.shape
    grid = (b, h, s // tq)
    nkb = vt.shape[2]
    return pl.pallas_call(
        functools.partial(_attn_kernel, tq=tq),
        grid=grid,
        in_specs=[
            pl.BlockSpec((None, None, QK_PAD, tq), lambda bi, hi, i: (bi, hi, 0, i)),
            pl.BlockSpec((None, None, s, QK_PAD), lambda bi, hi, i: (bi, hi, 0, 0)),
            pl.BlockSpec((None, None, nkb, D_V, _KV_BLOCK), lambda bi, hi, i: (bi, hi, 0, 0, 0)),
            pl.BlockSpec((None, LANES, QK_PAD), lambda bi, hi, i: (hi, 0, 0)),
            pl.BlockSpec((None, D_V, LANES), lambda bi, hi, i: (hi, 0, 0)),
        ],
        out_specs=pl.BlockSpec((None, tq, D_V), lambda bi, hi, i: (bi, i, hi)),
        out_shape=jax.ShapeDtypeStruct((b, s, h * D_V), BF16),
        scratch_shapes=[pltpu.VMEM((D_V + _DENOM_ROWS, tq), F32), pltpu.VMEM((_KV_BLOCK, tq), F32),
                        pltpu.VMEM((_KV_BLOCK, tq), F32)],
        compiler_params=pltpu.CompilerParams(
            dimension_semantics=("arbitrary", "arbitrary", "arbitrary"), vmem_limit_bytes=VMEM_LIMIT),
        name="attn",
    )(qt, k, vt, km, vmt)


def _outproj_kernel(x_ref, oa_ref, cn_ref, ln0g_ref, ln0b_ref, ag_ref, wo_ref, ln1g_ref, ln1b_ref,
                    wq_ref, sk_ref, h1_out, h1b_out, st_out):
    h0 = _layer_norm(x_ref[...], ln0g_ref[...], ln0b_ref[...])
    an = _rms_norm(oa_ref[...].astype(F32), ag_ref[...]).astype(BF16)
    mixed = _dot(an, wo_ref[0:ATTN_DIM, :]) + _dot(cn_ref[...], wo_ref[ATTN_DIM:, :])
    h1 = _layer_norm(DEEPNORM_ALPHA * h0 + mixed, ln1g_ref[...], ln1b_ref[...])
    h1_out[...] = h1
    h1b = h1.astype(BF16)
    h1b_out[...] = h1b
    pq = _dot(h1b, wq_ref[...])
    for hc in range(PEER_HEADS * 2):
        pqs = pq[:, hc * D_HALF:(hc + 1) * D_HALF].astype(BF16)
        st_out[hc * N_KEYS:(hc + 1) * N_KEYS, :] = _dot_nt(sk_ref[hc % 2], pqs)


def _outproj(x, oa, cn, ln0g, ln0b, ag, wo, ln1g, ln1b, wq, sk, *, tm):
    b, s, d = x.shape
    grid = (b, s // tm)
    nsc = PEER_HEADS * 2 * N_KEYS
    tok = lambda w: pl.BlockSpec((None, tm, w), lambda bi, i: (bi, i, 0))
    return pl.pallas_call(
        _outproj_kernel,
        grid=grid,
        in_specs=[tok(d), tok(oa.shape[2]), tok(cn.shape[2]),
                  _const_spec(ln0g.shape), _const_spec(ln0b.shape), _const_spec(ag.shape), _const_spec(wo.shape),
                  _const_spec(ln1g.shape), _const_spec(ln1b.shape), _const_spec(wq.shape), _const_spec(sk.shape)],
        out_specs=(tok(d), tok(d), pl.BlockSpec((None, nsc, tm), lambda bi, i: (bi, 0, i))),
        out_shape=(jax.ShapeDtypeStruct((b, s, d), F32), jax.ShapeDtypeStruct((b, s, d), BF16),
                   jax.ShapeDtypeStruct((b, nsc, s), F32)),
        compiler_params=pltpu.CompilerParams(
            dimension_semantics=("arbitrary", "arbitrary"), vmem_limit_bytes=VMEM_LIMIT),
        name="outproj",
    )(x, oa, cn, ln0g, ln0b, ag, wo, ln1g, ln1b, wq, sk)


_BIG_ID = 1 << 20


def _extract_top(s, ids, n):
    w = s.shape[1]
    rows = lax.broadcasted_iota(jnp.int32, (n, w), 0)
    vals = jnp.zeros((n, w), F32)
    sel = jnp.zeros((n, w), jnp.int32)
    val_rows = []
    for r in range(n):
        m = jnp.max(s, axis=0, keepdims=True)
        pick = jnp.min(jnp.where(s == m, ids, _BIG_ID), axis=0, keepdims=True)
        vals = jnp.where(rows == r, m, vals)
        sel = jnp.where(rows == r, pick, sel)
        val_rows.append(m)
        s = jnp.where(ids == pick, -jnp.inf, s)
    return vals, sel, val_rows


_SUBLANES = 8


def _top_keys(s_ref, base, n):
    w = s_ref.shape[1]
    nv = N_KEYS // _SUBLANES
    assert n <= nv
    sub = lax.broadcasted_iota(jnp.int32, (_SUBLANES, w), 0)
    vals = [s_ref[pl.ds(pl.multiple_of(base + r * _SUBLANES, _SUBLANES), _SUBLANES), :] for r in range(nv)]
    ids = [sub + r * _SUBLANES for r in range(nv)]
    for rnd in range(nv):
        for i in range(rnd % 2, nv - 1, 2):
            swap = vals[i + 1] > vals[i]
            hi, lo = jnp.maximum(vals[i], vals[i + 1]), jnp.minimum(vals[i], vals[i + 1])
            ids[i], ids[i + 1] = jnp.where(swap, ids[i + 1], ids[i]), jnp.where(swap, ids[i], ids[i + 1])
            vals[i], vals[i + 1] = hi, lo
    rows = lax.broadcasted_iota(jnp.int32, (n, w), 0)
    out_v = jnp.zeros((n, w), F32)
    out_i = jnp.zeros((n, w), jnp.int32)
    val_rows = []
    for t in range(n):
        m = jnp.max(vals[0], axis=0, keepdims=True)
        pick = jnp.min(jnp.where(vals[0] == m, ids[0], _BIG_ID), axis=0, keepdims=True)
        out_v = jnp.where(rows == t, m, out_v)
        out_i = jnp.where(rows == t, pick, out_i)
        val_rows.append(m)
        win = ids[0] == pick
        for r in range(n - 1 - t):
            vals[r] = jnp.where(win, vals[r + 1], vals[r])
            ids[r] = jnp.where(win, ids[r + 1], ids[r])
    return out_v, out_i, val_rows


def _route_kernel(st_ref, i1_out, i2_out, g_out):
    w = st_ref.shape[1]
    k = PEER_TOPK
    r16 = lax.broadcasted_iota(jnp.int32, (k, w), 0)
    r8 = lax.broadcasted_iota(jnp.int32, (8, w), 0)

    def head(h, _):
        base = pl.multiple_of(h * 2 * N_KEYS, 2 * N_KEYS)
        v1, i1, v1rows = _top_keys(st_ref, base, k)
        v2, i2, v2rows = _top_keys(st_ref, base + N_KEYS, k)
        pieces, piece_ids = [], []

        def add(vals, ids, ok):
            pieces.append(jnp.where(ok, vals, -jnp.inf))
            piece_ids.append(ids)

        add(v1rows[0] + v2, r16, r16 < k)
        for a in (1, 2, 3):
            add(v1rows[a] + v2[0:8, :], a * k + r8, (a + 1) * (r8 + 1) <= k)
        add(v1 + v2rows[0], r16 * k, r16 >= 4)
        for bb in (1, 2):
            add(v1[0:8, :] + v2rows[bb], r8 * k + bb, (r8 >= 4) & ((r8 + 1) * (bb + 1) <= k))
        cand = jnp.concatenate(pieces, axis=0)
        cand_ids = jnp.concatenate(piece_ids, axis=0)
        best, best_id, _ = _extract_top(cand, cand_ids, k)
        a_sel = best_id >> _TOPK_SHIFT
        b_sel = best_id & (k - 1)
        e1 = jnp.zeros((k, w), jnp.int32)
        e2 = jnp.zeros((k, w), jnp.int32)
        for c in range(k):
            e1 = jnp.where(a_sel == c, i1[c:c + 1, :], e1)
            e2 = jnp.where(b_sel == c, i2[c:c + 1, :], e2)
        ex = jnp.exp(best - best[0:1, :])
        gate = ex / jnp.sum(ex, axis=0, keepdims=True)
        o = pl.multiple_of(h * k, k)
        i1_out[pl.ds(o, k), :] = e1
        i2_out[pl.ds(o, k), :] = e2
        g_out[pl.ds(o, k), :] = gate
        return 0

    lax.fori_loop(0, PEER_HEADS, head, 0, unroll=2)


def _route(st, *, tl):
    b, nsc, s = st.shape
    nj = PEER_HEADS * PEER_TOPK
    grid = (b, s // tl)
    ospec = pl.BlockSpec((None, nj, tl), lambda bi, i: (bi, 0, i))
    return pl.pallas_call(
        _route_kernel,
        grid=grid,
        in_specs=[pl.BlockSpec((None, nsc, tl), lambda bi, i: (bi, 0, i))],
        out_specs=(ospec, ospec, ospec),
        out_shape=(jax.ShapeDtypeStruct((b, nj, s), jnp.int32), jax.ShapeDtypeStruct((b, nj, s), jnp.int32),
                   jax.ShapeDtypeStruct((b, nj, s), F32)),
        compiler_params=pltpu.CompilerParams(dimension_semantics=("arbitrary", "arbitrary")),
        name="route",
    )(st)


_GROUP = 16
_PITCH = N_KEYS + 4


def _scatter_kernel(i1_ref, i2_ref, g_ref, gd_out, stage_a, stage_b):
    tg = i1_ref.shape[0]
    nj = i1_ref.shape[1]
    sub = lax.broadcasted_iota(jnp.int32, (N_KEYS, nj), 0)
    stages = (stage_a, stage_b)

    def fill(grp):
        stage = stages[grp % 2]
        for t in range(_GROUP):
            tok = grp * _GROUP + t
            r1 = i1_ref[tok:tok + 1, :]
            r2 = i2_ref[tok:tok + 1, :]
            gg = g_ref[tok:tok + 1, :]
            p1 = jnp.where(sub == r1, gg, 0.0).astype(BF16)
            p2 = jnp.where(sub == r2, 1.0, 0.0).astype(BF16)
            stage[t * _PITCH:t * _PITCH + N_KEYS, :] = _dot_nt(p1, p2)

    def drain(grp):
        stage = stages[grp % 2]
        for a in range(N_KEYS):
            rows = stage[pl.ds(a, _GROUP, stride=_PITCH), :]
            gd_out[grp * _GROUP:(grp + 1) * _GROUP, a * N_KEYS:(a + 1) * N_KEYS] = rows.astype(BF16)

    ngroups = tg // _GROUP
    fill(0)
    for grp in range(ngroups):
        if grp + 1 < ngroups:
            fill(grp + 1)
        drain(grp)


def _scatter(i1, i2, g, *, tg):
    t, nj = i1.shape
    ne = N_KEYS * N_KEYS
    spec = pl.BlockSpec((tg, nj), lambda i: (i, 0))
    return pl.pallas_call(
        _scatter_kernel,
        grid=(t // tg,),
        in_specs=[spec, spec, spec],
        out_specs=pl.BlockSpec((tg, ne), lambda i: (i, 0)),
        out_shape=jax.ShapeDtypeStruct((t, ne), BF16),
        scratch_shapes=[pltpu.VMEM((_GROUP * _PITCH, N_KEYS), F32), pltpu.VMEM((_GROUP * _PITCH, N_KEYS), F32)],
        compiler_params=pltpu.CompilerParams(dimension_semantics=("arbitrary",)),
        name="scatter",
    )(i1, i2, g)


def _peer_kernel(xb_ref, h1_ref, ut_ref, v_ref, gd_ref, ln2g_ref, ln2b_ref, o_ref, acc):
    e = pl.program_id(1)

    @pl.when(e == 0)
    def _():
        acc[...] = jnp.zeros_like(acc)

    a = _dot(xb_ref[...], ut_ref[...])
    act = 0.5 * a * (1.0 + lax.erf(a * (2.0 ** -0.5)))
    hd = (act * gd_ref[...].astype(F32)).astype(BF16)
    acc[...] += _dot(hd, v_ref[...])

    @pl.when(e == pl.num_programs(1) - 1)
    def _():
        o_ref[...] = _layer_norm(DEEPNORM_ALPHA * h1_ref[...] + acc[...], ln2g_ref[...], ln2b_ref[...])


def _peer(xb, h1, ut, v, gd, ln2g, ln2b, *, tm, ec):
    t, d = xb.shape
    ne = v.shape[0]
    grid = (t // tm, ne // ec)
    return pl.pallas_call(
        _peer_kernel,
        grid=grid,
        in_specs=[
            pl.BlockSpec((tm, d), lambda i, e: (i, 0)),
            pl.BlockSpec((tm, d), lambda i, e: (i, 0)),
            pl.BlockSpec((d, ec), lambda i, e: (0, e)),
            pl.BlockSpec((ec, d), lambda i, e: (e, 0)),
            pl.BlockSpec((tm, ec), lambda i, e: (i, e)),
            _const_spec(ln2g.shape), _const_spec(ln2b.shape),
        ],
        out_specs=pl.BlockSpec((tm, d), lambda i, e: (i, 0)),
        out_shape=jax.ShapeDtypeStruct((t, d), F32),
        scratch_shapes=[pltpu.VMEM((tm, d), F32)],
        compiler_params=pltpu.CompilerParams(
            dimension_semantics=("arbitrary", "arbitrary"), vmem_limit_bytes=VMEM_LIMIT),
        name="peer",
    )(xb, h1, ut, v, gd, ln2g, ln2b)


def _rope_tables(first_pos, n):
    inv = 1.0 / (ROPE_THETA ** (jnp.arange(0, D_ROPE, 2, dtype=F32) / D_ROPE))
    ang = (first_pos + jnp.arange(n, dtype=F32))[:, None] * inv[None, :]
    cos, sin = jnp.cos(ang), jnp.sin(ang)
    z = jnp.zeros((n, LANES - D_ROPE), F32)
    return jnp.concatenate([cos, cos, z], axis=1), jnp.concatenate([-sin, sin, z], axis=1)


def _swap_halves(w):
    half = w.shape[-1] // 2
    return jnp.concatenate([w[..., half:], w[..., :half]], axis=-1)


def _pick(n, prefs):
    for p in prefs:
        if n % p == 0:
            return p
    raise ValueError(f"no tile in {prefs} divides {n}")


def kernel(x, meta_tokens, ln0_g, ln0_b, w_in, q_norm_g, kv_norm_g, w_uq, w_ukv, conv_w, attn_out_g, conv_out_g,
           w_o, ln1_g, ln1_b, peer_w_query, peer_sub_keys, peer_u, peer_v, ln2_g, ln2_b):
    b, s, d = x.shape
    assert w_in.shape[0] == 1 and s % _Q_TILE == 0
    conv_dim = conv_w.shape[2]
    h = MLA_HEADS
    row = lambda a: a.reshape(1, -1).astype(F32)

    wi = w_in[0]
    o_kv, o_kr, o_b = Q_RANK, Q_RANK + KV_RANK, Q_RANK + KV_RANK + D_ROPE
    w_kr = wi[:, o_kr:o_b]
    zpad = jnp.zeros((d, LANES - D_ROPE), F32)
    win = jnp.concatenate(
        [wi[:, :o_kr], w_kr, zpad, _swap_halves(w_kr), zpad, wi[:, o_b:]], axis=1).astype(BF16)
    wq3 = w_uq[0].reshape(Q_RANK, h, D_QK)
    zq = jnp.zeros((Q_RANK, h, QK_PAD - D_QK), F32)
    wqat = jnp.concatenate([wq3, zq], axis=2).reshape(Q_RANK, h * QK_PAD).T.astype(BF16)
    wqbt = jnp.concatenate([_swap_halves(wq3[:, :, D_NOPE:]), zq], axis=2).reshape(Q_RANK, h * LANES).T.astype(BF16)
    wkv3 = w_ukv[0].reshape(KV_RANK, h, D_NOPE + D_V)
    wk = wkv3[:, :, :D_NOPE].reshape(KV_RANK, h * D_NOPE).astype(BF16)
    wvt = wkv3[:, :, D_NOPE:].reshape(KV_RANK, h * D_V).T.astype(BF16)
    wo = w_o[0].astype(BF16)
    wpq = peer_w_query[0].astype(BF16)
    sk = peer_sub_keys[0].astype(BF16)
    ut = peer_u[0].T.astype(BF16)
    pv = peer_v[0].astype(BF16)

    inproj_w = (row(ln0_g), row(ln0_b), win, row(q_norm_g[0]), row(kv_norm_g[0]), wqat, wqbt, wk, wvt,
                conv_w[0].astype(F32), row(conv_out_g[0]))

    t1m, t2m = _rope_tables(0, N_META)
    _, km, vmt, _, utail = _inproj(meta_tokens[None].astype(F32), t1m, t2m, t1m.T, t2m.T,
                                   jnp.zeros((8, conv_dim), F32), *inproj_w, tm=N_META)
    km = jnp.pad(km[0], ((0, 0), (0, LANES - N_META), (0, 0)))
    vmt = jnp.pad(vmt[0, :, 0], ((0, 0), (0, 0), (0, LANES - N_META)))

    t1, t2 = _rope_tables(N_META, s)
    qt, k, vt, convn, _ = _inproj(x, t1, t2, t1.T, t2.T, utail[0], *inproj_w, tm=_KV_BLOCK)
    oa = _attention(qt, k, vt, km, vmt, tq=_Q_TILE)
    h1, h1b, st = _outproj(x, oa, convn, row(ln0_g), row(ln0_b), row(attn_out_g[0]), wo, row(ln1_g[0]),
                           row(ln1_b[0]), wpq, sk, tm=_pick(s, (256, 128)))
    i1, i2, g = _route(st, tl=LANES)
    tok = lambda a: jnp.swapaxes(a, 1, 2).reshape(b * s, a.shape[1])
    gd = _scatter(tok(i1), tok(i2), tok(g), tg=_pick(b * s, (128, 64)))
    out = _peer(h1b.reshape(b * s, d), h1.reshape(b * s, d), ut, pv, gd, row(ln2_g[0]), row(ln2_b[0]),
                tm=_pick(b * s, (512, 256, 128)), ec=1024)
    return out.reshape(b, s, d)
```

```python
import functools

import jax
import jax.numpy as jnp
import numpy as np
from jax import lax
from jax.experimental import pallas as pl
from jax.experimental.pallas import tpu as pltpu

CHUNK = 64
N_META = 16
MLA_HEADS = 8
D_NOPE = 128
D_ROPE = 64
D_QK = D_NOPE + D_ROPE
D_V = 128
Q_RANK = 384
KV_RANK = 512
ROPE_THETA = 10000.0
ATTN_DIM = MLA_HEADS * D_V
CONV_WIDTH = 3
PEER_HEADS = 8
N_KEYS = 128
D_HALF = 128
PEER_TOPK = 16
DEPTH = 1
DEEPNORM_ALPHA = (2.0 * DEPTH) ** 0.25
EPS = 1e-5
NEG_INF = -1e30

_CHUNK_SHIFT = CHUNK.bit_length() - 1
_TOPK_SHIFT = PEER_TOPK.bit_length() - 1
assert 1 << _CHUNK_SHIFT == CHUNK and 1 << _TOPK_SHIFT == PEER_TOPK

LANES = 128
QK_PAD = 256
_KV_BLOCK = 512
_Q_BLOCK = 256
_Q_TILE = 2 * _KV_BLOCK
_DENOM_ROWS = 16
_LOG2E = 1.4426950408889634
VMEM_LIMIT = 58 * 1024 * 1024

F32 = jnp.float32
BF16 = jnp.bfloat16
F8 = jnp.float8_e4m3fn
_F8_TARGET = 224.0


def _dot(a, b):
    return jnp.dot(a, b, preferred_element_type=F32)


def _dot_nt(a, b):
    return lax.dot_general(a, b, (((1,), (1,)), ((), ())), preferred_element_type=F32)


def _layer_norm(x, g, b):
    mu = jnp.mean(x, axis=-1, keepdims=True)
    xc = x - mu
    var = jnp.mean(xc * xc, axis=-1, keepdims=True)
    return xc * lax.rsqrt(var + EPS) * g + b


def _rms_norm(x, g):
    return x * lax.rsqrt(jnp.mean(x * x, axis=-1, keepdims=True) + EPS) * g


def _const_spec(shape):
    nd = len(shape)
    return pl.BlockSpec(shape, lambda *_: (0,) * nd, pipeline_mode=pl.Buffered(1))


_OFF_CQ = 0
_OFF_CKV = _OFF_CQ + Q_RANK
_OFF_KRA = _OFF_CKV + KV_RANK
_OFF_KRB = _OFF_KRA + LANES
_OFF_BG = _OFF_KRB + LANES


def _inproj_kernel(x_ref, t1_ref, t2_ref, t1t_ref, t2t_ref, uinit_ref, ln0g_ref, ln0b_ref, win_ref, qg_ref,
                   kvg_ref, wqat_ref, wqbt_ref, wk_ref, wvt_ref, convw_ref, convg_ref,
                   qt_out, k_out, vt_out, convn_out, utail_out, ubuf, *, tm, vb, conv_dim):
    i = pl.program_id(1)
    off_cg = _OFF_BG + conv_dim
    off_hc = off_cg + conv_dim

    @pl.when(i == 0)
    def _():
        ubuf[0:8, :] = uinit_ref[...]

    hb = _layer_norm(x_ref[...], ln0g_ref[...], ln0b_ref[...]).astype(BF16)
    t1 = t1_ref[...]
    t2 = t2_ref[...]
    t1t = t1t_ref[...]
    t2t = t2t_ref[...]
    qscale = D_QK ** -0.5 * _LOG2E

    cq = _dot(hb, win_ref[:, _OFF_CQ:_OFF_CQ + Q_RANK])
    ckv = _dot(hb, win_ref[:, _OFF_CKV:_OFF_CKV + KV_RANK])
    kra = _dot(hb, win_ref[:, _OFF_KRA:_OFF_KRA + LANES])
    krb = _dot(hb, win_ref[:, _OFF_KRB:_OFF_KRB + LANES])
    krot = (kra * t1 + krb * t2).astype(BF16)
    cqn = _rms_norm(cq, qg_ref[...]).astype(BF16)
    ckvn = _rms_norm(ckv, kvg_ref[...]).astype(BF16)

    for h in range(MLA_HEADS):
        qat = _dot_nt(wqat_ref[h * QK_PAD:(h + 1) * QK_PAD, :], cqn)
        qbt = _dot_nt(wqbt_ref[h * LANES:(h + 1) * LANES, :], cqn)
        qt_out[h, 0:LANES, :] = (qat[0:LANES, :] * qscale).astype(BF16)
        qt_out[h, LANES:QK_PAD, :] = ((qat[LANES:QK_PAD, :] * t1t + qbt * t2t) * qscale).astype(BF16)
        k_out[h, :, 0:LANES] = _dot(ckvn, wk_ref[:, h * LANES:(h + 1) * LANES]).astype(BF16)
        k_out[h, :, LANES:QK_PAD] = krot
        vt = _dot_nt(wvt_ref[h * D_V:(h + 1) * D_V, :], ckvn).astype(BF16)
        for bk in range(tm // vb):
            vt_out[h, bk] = vt[:, bk * vb:(bk + 1) * vb]

    bg = _dot(hb, win_ref[:, _OFF_BG:_OFF_BG + conv_dim])
    cg = _dot(hb, win_ref[:, off_cg:off_cg + conv_dim])
    hc = _dot(hb, win_ref[:, off_hc:off_hc + conv_dim])
    u = cg * hc
    ubuf[8:8 + tm, :] = u
    u1 = ubuf[7:7 + tm, :]
    u2 = ubuf[6:6 + tm, :]
    w = convw_ref[...]
    y = bg * (u2 * w[0:1, :] + u1 * w[1:2, :] + u * w[2:3, :])
    convn_out[...] = _rms_norm(y, convg_ref[...]).astype(BF16)
    tail = ubuf[tm:tm + 8, :]
    utail_out[...] = tail
    ubuf[0:8, :] = tail


def _inproj(x, t1, t2, t1t, t2t, uinit, ln0g, ln0b, win, qg, kvg, wqat, wqbt, wk, wvt, convw, convg, *, tm):
    b, s, d = x.shape
    conv_dim = convw.shape[1]
    grid = (b, s // tm)
    h = MLA_HEADS
    vb = min(tm, _KV_BLOCK)
    out_shape = (
        jax.ShapeDtypeStruct((b, h, QK_PAD, s), BF16),
        jax.ShapeDtypeStruct((b, h, s, QK_PAD), BF16),
        jax.ShapeDtypeStruct((b, h, s // vb, D_V, vb), BF16),
        jax.ShapeDtypeStruct((b, s, conv_dim), BF16),
        jax.ShapeDtypeStruct((b, 8, conv_dim), F32),
    )
    in_specs = [
        pl.BlockSpec((None, tm, d), lambda bi, i: (bi, i, 0)),
        pl.BlockSpec((tm, LANES), lambda bi, i: (i, 0)),
        pl.BlockSpec((tm, LANES), lambda bi, i: (i, 0)),
        pl.BlockSpec((LANES, tm), lambda bi, i: (0, i)),
        pl.BlockSpec((LANES, tm), lambda bi, i: (0, i)),
        _const_spec(uinit.shape), _const_spec(ln0g.shape), _const_spec(ln0b.shape), _const_spec(win.shape),
        _const_spec(qg.shape), _const_spec(kvg.shape), _const_spec(wqat.shape), _const_spec(wqbt.shape),
        _const_spec(wk.shape), _const_spec(wvt.shape), _const_spec(convw.shape), _const_spec(convg.shape),
    ]
    out_specs = (
        pl.BlockSpec((None, h, QK_PAD, tm), lambda bi, i: (bi, 0, 0, i)),
        pl.BlockSpec((None, h, tm, QK_PAD), lambda bi, i: (bi, 0, i, 0)),
        pl.BlockSpec((None, h, tm // vb, D_V, vb), lambda bi, i: (bi, 0, i, 0, 0)),
        pl.BlockSpec((None, tm, conv_dim), lambda bi, i: (bi, i, 0)),
        pl.BlockSpec((None, 8, conv_dim), lambda bi, i: (bi, 0, 0)),
    )
    return pl.pallas_call(
        functools.partial(_inproj_kernel, tm=tm, vb=vb, conv_dim=conv_dim),
        grid=grid, in_specs=in_specs, out_specs=out_specs, out_shape=out_shape,
        scratch_shapes=[pltpu.VMEM((tm + 8, conv_dim), F32)],
        compiler_params=pltpu.CompilerParams(
            dimension_semantics=("arbitrary", "arbitrary"), vmem_limit_bytes=VMEM_LIMIT),
        name="inproj",
    )(x, t1, t2, t1t, t2t, uinit, ln0g, ln0b, win, qg, kvg, wqat, wqbt, wk, wvt, convw, convg)


def _attn_kernel(qt_ref, k_ref, vt_ref, km_ref, vmt_ref, o_ref, acc_ref, sa_ref, sb_ref, *, tq):
    i = pl.program_id(2)
    kb, qb = _KV_BLOCK, _Q_BLOCK
    ncb = tq // qb
    nd = tq // kb
    assert nd == 2, "the two-buffer pipeline consumes key blocks in pairs"

    def cols(cb):
        return slice(cb * qb, (cb + 1) * qb)

    def scores(j, first_cb=0):
        k = k_ref[pl.ds(pl.multiple_of(j * kb, kb), kb), :]
        return _dot(k, qt_ref[:, first_cb * qb:])

    def with_ones(vt):
        return jnp.concatenate([vt, jnp.ones((_DENOM_ROWS, vt.shape[1]), BF16)], axis=0)

    def absorb(cb, m, s, smax, vt1):
        mn = jnp.maximum(m, smax)
        a = jnp.exp2(m - mn)
        p = jnp.exp2(s - mn)
        acc_ref[:, cols(cb)] = a * acc_ref[:, cols(cb)] + _dot(vt1, p.astype(BF16))
        return mn

    def stash(s_ref, k, cb):
        s = _dot(k, qt_ref[:, cols(cb)])
        s_ref[:, cols(cb)] = s
        return jnp.max(s, axis=0, keepdims=True)

    def key_block(j):
        return k_ref[pl.ds(pl.multiple_of(j * kb, kb), kb), :]

    def trade(carry, s_old, smax_old, vt_old, s_new, j_new):
        k_new = key_block(j_new)
        vt1 = with_ones(vt_old)
        ms, smax_new = list(carry), []
        for cb in range(ncb):
            s = s_old[:, cols(cb)]
            smax_new.append(stash(s_new, k_new, cb))
            ms[cb] = absorb(cb, ms[cb], s, smax_old[cb], vt1)
        return tuple(ms), tuple(smax_new)

    k0 = key_block(0)
    max_a = tuple(stash(sa_ref, k0, cb) for cb in range(ncb))

    ms = []
    meta_rows = lax.broadcasted_iota(jnp.int32, (km_ref.shape[0], qb), 0) < N_META
    vmt1 = with_ones(vmt_ref[...])
    for cb in range(ncb):
        s = jnp.where(meta_rows, _dot(km_ref[...], qt_ref[:, cols(cb)]), NEG_INF)
        m0 = jnp.max(s, axis=0, keepdims=True)
        ms.append(m0)
        acc_ref[:, cols(cb)] = _dot(vmt1, jnp.exp2(s - m0).astype(BF16))

    def body(jp, carry):
        stats, max_a = carry
        j = 2 * jp
        stats, max_b = trade(stats, sa_ref, max_a, vt_ref[j], sb_ref, j + 1)
        return trade(stats, sb_ref, max_b, vt_ref[j + 1], sa_ref, j + 2)

    ms, _ = lax.fori_loop(0, i, body, (tuple(ms), max_a))
    ms = list(ms)

    for jj in range(nd):
        j = i * nd + jj
        first_cb = (jj * kb) // qb
        s_all = None if jj == 0 else scores(j, first_cb)
        vt1 = with_ones(vt_ref[j])
        for cb in range(first_cb, ncb):
            s = sa_ref[:, cols(cb)] if jj == 0 else s_all[:, (cb - first_cb) * qb:(cb - first_cb + 1) * qb]
            if (jj + 1) * kb > cb * qb + CHUNK:
                key_chunk = (lax.broadcasted_iota(jnp.int32, (kb, qb), 0) + jj * kb) >> _CHUNK_SHIFT
                qry_chunk = (lax.broadcasted_iota(jnp.int32, (kb, qb), 1) + cb * qb) >> _CHUNK_SHIFT
                s = jnp.where(key_chunk <= qry_chunk, s, NEG_INF)
            ms[cb] = absorb(cb, ms[cb], s, jnp.max(s, axis=0, keepdims=True), vt1)

    for cb in range(ncb):
        o = acc_ref[0:D_V, cols(cb)] / acc_ref[D_V:D_V + 1, cols(cb)]
        o_ref[cb * qb:(cb + 1) * qb, :] = o.T.astype(BF16)


def _attention(qt, k, vt, km, vmt, *, tq):
    b, h, _, s = qt.shape
    grid = (b, h, s // tq)
    nkb = vt.shape[2]
    return pl.pallas_call(
        functools.partial(_attn_kernel, tq=tq),
        grid=grid,
        in_specs=[
            pl.BlockSpec((None, None, QK_PAD, tq), lambda bi, hi, i: (bi, hi, 0, i)),
            pl.BlockSpec((None, None, s, QK_PAD), lambda bi, hi, i: (bi, hi, 0, 0)),
            pl.BlockSpec((None, None, nkb, D_V, _KV_BLOCK), lambda bi, hi, i: (bi, hi, 0, 0, 0)),
            pl.BlockSpec((None, LANES, QK_PAD), lambda bi, hi, i: (hi, 0, 0)),
            pl.BlockSpec((None, D_V, LANES), lambda bi, hi, i: (hi, 0, 0)),
        ],
        out_specs=pl.BlockSpec((None, tq, D_V), lambda bi, hi, i: (bi, i, hi)),
        out_shape=jax.ShapeDtypeStruct((b, s, h * D_V), BF16),
        scratch_shapes=[pltpu.VMEM((D_V + _DENOM_ROWS, tq), F32), pltpu.VMEM((_KV_BLOCK, tq), F32),
                        pltpu.VMEM((_KV_BLOCK, tq), F32)],
        compiler_params=pltpu.CompilerParams(
            dimension_semantics=("arbitrary", "arbitrary", "arbitrary"), vmem_limit_bytes=VMEM_LIMIT),
        name="attn",
    )(qt, k, vt, km, vmt)


def _outproj_kernel(xscale_ref, x_ref, oa_ref, cn_ref, ln0g_ref, ln0b_ref, ag_ref, wo_ref, ln1g_ref, ln1b_ref,
                    wq_ref, sk_ref, h1_out, h1q_out, st_out):
    h0 = _layer_norm(x_ref[...], ln0g_ref[...], ln0b_ref[...])
    an = _rms_norm(oa_ref[...].astype(F32), ag_ref[...]).astype(BF16)
    mixed = _dot(an, wo_ref[0:ATTN_DIM, :]) + _dot(cn_ref[...], wo_ref[ATTN_DIM:, :])
    h1 = _layer_norm(DEEPNORM_ALPHA * h0 + mixed, ln1g_ref[...], ln1b_ref[...])
    h1_out[...] = h1
    h1q_out[...] = (h1 * xscale_ref[0]).astype(F8)
    h1b = h1.astype(BF16)
    pq = _dot(h1b, wq_ref[...])
    for hc in range(PEER_HEADS * 2):
        pqs = pq[:, hc * D_HALF:(hc + 1) * D_HALF].astype(BF16)
        st_out[hc * N_KEYS:(hc + 1) * N_KEYS, :] = _dot_nt(sk_ref[hc % 2], pqs)


def _outproj(xscale, x, oa, cn, ln0g, ln0b, ag, wo, ln1g, ln1b, wq, sk, *, tm):
    b, s, d = x.shape
    grid = (b, s // tm)
    nsc = PEER_HEADS * 2 * N_KEYS
    tok = lambda w: pl.BlockSpec((None, tm, w), lambda bi, i: (bi, i, 0))
    return pl.pallas_call(
        _outproj_kernel,
        grid=grid,
        in_specs=[pl.BlockSpec(memory_space=pltpu.SMEM), tok(d), tok(oa.shape[2]), tok(cn.shape[2]),
                  _const_spec(ln0g.shape), _const_spec(ln0b.shape), _const_spec(ag.shape), _const_spec(wo.shape),
                  _const_spec(ln1g.shape), _const_spec(ln1b.shape), _const_spec(wq.shape), _const_spec(sk.shape)],
        out_specs=(tok(d), tok(d), pl.BlockSpec((None, nsc, tm), lambda bi, i: (bi, 0, i))),
        out_shape=(jax.ShapeDtypeStruct((b, s, d), F32), jax.ShapeDtypeStruct((b, s, d), F8),
                   jax.ShapeDtypeStruct((b, nsc, s), F32)),
        compiler_params=pltpu.CompilerParams(
            dimension_semantics=("arbitrary", "arbitrary"), vmem_limit_bytes=VMEM_LIMIT),
        name="outproj",
    )(xscale, x, oa, cn, ln0g, ln0b, ag, wo, ln1g, ln1b, wq, sk)


_BIG_ID = 1 << 20


def _extract_top(s, ids, n):
    w = s.shape[1]
    rows = lax.broadcasted_iota(jnp.int32, (n, w), 0)
    vals = jnp.zeros((n, w), F32)
    sel = jnp.zeros((n, w), jnp.int32)
    val_rows = []
    for r in range(n):
        m = jnp.max(s, axis=0, keepdims=True)
        pick = jnp.min(jnp.where(s == m, ids, _BIG_ID), axis=0, keepdims=True)
        vals = jnp.where(rows == r, m, vals)
        sel = jnp.where(rows == r, pick, sel)
        val_rows.append(m)
        s = jnp.where(ids == pick, -jnp.inf, s)
    return vals, sel, val_rows


_SUBLANES = 8


def _top_keys(s_ref, base, n):
    w = s_ref.shape[1]
    nv = N_KEYS // _SUBLANES
    assert n <= nv
    sub = lax.broadcasted_iota(jnp.int32, (_SUBLANES, w), 0)
    vals = [s_ref[pl.ds(pl.multiple_of(base + r * _SUBLANES, _SUBLANES), _SUBLANES), :] for r in range(nv)]
    ids = [sub + r * _SUBLANES for r in range(nv)]
    for rnd in range(nv):
        for i in range(rnd % 2, nv - 1, 2):
            swap = vals[i + 1] > vals[i]
            hi, lo = jnp.maximum(vals[i], vals[i + 1]), jnp.minimum(vals[i], vals[i + 1])
            ids[i], ids[i + 1] = jnp.where(swap, ids[i + 1], ids[i]), jnp.where(swap, ids[i], ids[i + 1])
            vals[i], vals[i + 1] = hi, lo
    rows = lax.broadcasted_iota(jnp.int32, (n, w), 0)
    out_v = jnp.zeros((n, w), F32)
    out_i = jnp.zeros((n, w), jnp.int32)
    val_rows = []
    for t in range(n):
        m = jnp.max(vals[0], axis=0, keepdims=True)
        pick = jnp.min(jnp.where(vals[0] == m, ids[0], _BIG_ID), axis=0, keepdims=True)
        out_v = jnp.where(rows == t, m, out_v)
        out_i = jnp.where(rows == t, pick, out_i)
        val_rows.append(m)
        win = ids[0] == pick
        for r in range(n - 1 - t):
            vals[r] = jnp.where(win, vals[r + 1], vals[r])
            ids[r] = jnp.where(win, ids[r + 1], ids[r])
    return out_v, out_i, val_rows


def _route_kernel(st_ref, i1_out, i2_out, g_out):
    w = st_ref.shape[1]
    k = PEER_TOPK
    r16 = lax.broadcasted_iota(jnp.int32, (k, w), 0)
    r8 = lax.broadcasted_iota(jnp.int32, (8, w), 0)

    def head(h, _):
        base = pl.multiple_of(h * 2 * N_KEYS, 2 * N_KEYS)
        v1, i1, v1rows = _top_keys(st_ref, base, k)
        v2, i2, v2rows = _top_keys(st_ref, base + N_KEYS, k)
        pieces, piece_ids = [], []

        def add(vals, ids, ok):
            pieces.append(jnp.where(ok, vals, -jnp.inf))
            piece_ids.append(ids)

        add(v1rows[0] + v2, r16, r16 < k)
        for a in (1, 2, 3):
            add(v1rows[a] + v2[0:8, :], a * k + r8, (a + 1) * (r8 + 1) <= k)
        add(v1 + v2rows[0], r16 * k, r16 >= 4)
        for bb in (1, 2):
            add(v1[0:8, :] + v2rows[bb], r8 * k + bb, (r8 >= 4) & ((r8 + 1) * (bb + 1) <= k))
        cand = jnp.concatenate(pieces, axis=0)
        cand_ids = jnp.concatenate(piece_ids, axis=0)
        best, best_id, _ = _extract_top(cand, cand_ids, k)
        a_sel = best_id >> _TOPK_SHIFT
        b_sel = best_id & (k - 1)
        e1 = jnp.zeros((k, w), jnp.int32)
        e2 = jnp.zeros((k, w), jnp.int32)
        for c in range(k):
            e1 = jnp.where(a_sel == c, i1[c:c + 1, :], e1)
            e2 = jnp.where(b_sel == c, i2[c:c + 1, :], e2)
        ex = jnp.exp(best - best[0:1, :])
        gate = ex / jnp.sum(ex, axis=0, keepdims=True)
        o = pl.multiple_of(h * k, k)
        i1_out[pl.ds(o, k), :] = e1
        i2_out[pl.ds(o, k), :] = e2
        g_out[pl.ds(o, k), :] = gate
        return 0

    lax.fori_loop(0, PEER_HEADS, head, 0, unroll=2)


def _route(st, *, tl):
    b, nsc, s = st.shape
    nj = PEER_HEADS * PEER_TOPK
    grid = (b, s // tl)
    ospec = pl.BlockSpec((None, nj, tl), lambda bi, i: (bi, 0, i))
    return pl.pallas_call(
        _route_kernel,
        grid=grid,
        in_specs=[pl.BlockSpec((None, nsc, tl), lambda bi, i: (bi, 0, i))],
        out_specs=(ospec, ospec, ospec),
        out_shape=(jax.ShapeDtypeStruct((b, nj, s), jnp.int32), jax.ShapeDtypeStruct((b, nj, s), jnp.int32),
                   jax.ShapeDtypeStruct((b, nj, s), F32)),
        compiler_params=pltpu.CompilerParams(dimension_semantics=("arbitrary", "arbitrary")),
        name="route",
    )(st)


_GROUP = 16
_PITCH = N_KEYS + 4


def _scatter_kernel(i1_ref, i2_ref, g_ref, gd_out, stage_a, stage_b):
    tg = i1_ref.shape[0]
    nj = i1_ref.shape[1]
    sub = lax.broadcasted_iota(jnp.int32, (N_KEYS, nj), 0)
    stages = (stage_a, stage_b)

    def fill(grp):
        stage = stages[grp % 2]
        for t in range(_GROUP):
            tok = grp * _GROUP + t
            r1 = i1_ref[tok:tok + 1, :]
            r2 = i2_ref[tok:tok + 1, :]
            gg = g_ref[tok:tok + 1, :]
            p1 = jnp.where(sub == r1, gg, 0.0).astype(BF16)
            p2 = jnp.where(sub == r2, 1.0, 0.0).astype(BF16)
            stage[t * _PITCH:t * _PITCH + N_KEYS, :] = _dot_nt(p1, p2)

    def drain(grp):
        stage = stages[grp % 2]
        for a in range(N_KEYS):
            rows = stage[pl.ds(a, _GROUP, stride=_PITCH), :]
            gd_out[grp * _GROUP:(grp + 1) * _GROUP, a * N_KEYS:(a + 1) * N_KEYS] = rows.astype(BF16)

    ngroups = tg // _GROUP
    fill(0)
    for grp in range(ngroups):
        if grp + 1 < ngroups:
            fill(grp + 1)
        drain(grp)


def _scatter(i1, i2, g, *, tg):
    t, nj = i1.shape
    ne = N_KEYS * N_KEYS
    spec = pl.BlockSpec((tg, nj), lambda i: (i, 0))
    return pl.pallas_call(
        _scatter_kernel,
        grid=(t // tg,),
        in_specs=[spec, spec, spec],
        out_specs=pl.BlockSpec((tg, ne), lambda i: (i, 0)),
        out_shape=jax.ShapeDtypeStruct((t, ne), BF16),
        scratch_shapes=[pltpu.VMEM((_GROUP * _PITCH, N_KEYS), F32), pltpu.VMEM((_GROUP * _PITCH, N_KEYS), F32)],
        compiler_params=pltpu.CompilerParams(dimension_semantics=("arbitrary",)),
        name="scatter",
    )(i1, i2, g)


def _peer_kernel(unscale_ref, xq_ref, h1_ref, ut_ref, v_ref, gd_ref, ln2g_ref, ln2b_ref, o_ref):
    e = pl.program_id(1)

    @pl.when(e == 0)
    def _():
        o_ref[...] = jnp.zeros_like(o_ref)

    a = _dot(xq_ref[...], ut_ref[...]) * unscale_ref[0]
    act = 0.5 * a * (1.0 + lax.erf(a * (2.0 ** -0.5)))
    hd = (act * gd_ref[...].astype(F32)).astype(F8)
    o_ref[...] += _dot(hd, v_ref[...])

    @pl.when(e == pl.num_programs(1) - 1)
    def _():
        f = o_ref[...] * unscale_ref[1]
        o_ref[...] = _layer_norm(DEEPNORM_ALPHA * h1_ref[...] + f, ln2g_ref[...], ln2b_ref[...])


def _peer(unscale, xq, h1, ut, v, gd, ln2g, ln2b, *, tm, ec):
    t, d = xq.shape
    ne = v.shape[0]
    grid = (t // tm, ne // ec)
    return pl.pallas_call(
        _peer_kernel,
        grid=grid,
        in_specs=[
            pl.BlockSpec(memory_space=pltpu.SMEM),
            pl.BlockSpec((tm, d), lambda i, e: (i, 0)),
            pl.BlockSpec((tm, d), lambda i, e: (i, 0), pipeline_mode=pl.Buffered(1)),
            pl.BlockSpec((d, ec), lambda i, e: (0, e)),
            pl.BlockSpec((ec, d), lambda i, e: (e, 0)),
            pl.BlockSpec((tm, ec), lambda i, e: (i, e)),
            _const_spec(ln2g.shape), _const_spec(ln2b.shape),
        ],
        out_specs=pl.BlockSpec((tm, d), lambda i, e: (i, 0)),
        out_shape=jax.ShapeDtypeStruct((t, d), F32),
        compiler_params=pltpu.CompilerParams(
            dimension_semantics=("arbitrary", "arbitrary"), vmem_limit_bytes=VMEM_LIMIT),
        name="peer",
    )(unscale, xq, h1, ut, v, gd, ln2g, ln2b)


def _pow2_below(x):
    return jnp.exp2(jnp.floor(jnp.log2(x)))


def _peer_scales(ln1_g, ln1_b, peer_u, peer_v, d):
    tiny = jnp.float32(1e-30)
    h_elem = jnp.max(jnp.sqrt(float(d)) * jnp.abs(ln1_g) + jnp.abs(ln1_b))
    h_norm = jnp.sqrt(float(d)) * jnp.max(jnp.abs(ln1_g)) + jnp.sqrt(jnp.sum(ln1_b * ln1_b))
    u_row = jnp.sqrt(jnp.max(jnp.sum(peer_u * peer_u, axis=1)))
    sx = _pow2_below(_F8_TARGET / jnp.maximum(h_elem, tiny))
    su = _pow2_below(_F8_TARGET / jnp.maximum(jnp.max(jnp.abs(peer_u)), tiny))
    sv = _pow2_below(_F8_TARGET / jnp.maximum(jnp.max(jnp.abs(peer_v)), tiny))
    sh = _pow2_below(_F8_TARGET / jnp.maximum(h_norm * u_row, tiny))
    return sx, su, sv, sh


def _rope_tables(first_pos, n):
    inv = 1.0 / (ROPE_THETA ** (jnp.arange(0, D_ROPE, 2, dtype=F32) / D_ROPE))
    ang = (first_pos + jnp.arange(n, dtype=F32))[:, None] * inv[None, :]
    cos, sin = jnp.cos(ang), jnp.sin(ang)
    z = jnp.zeros((n, LANES - D_ROPE), F32)
    return jnp.concatenate([cos, cos, z], axis=1), jnp.concatenate([-sin, sin, z], axis=1)


def _swap_halves(w):
    half = w.shape[-1] // 2
    return jnp.concatenate([w[..., half:], w[..., :half]], axis=-1)


def _pick(n, prefs):
    for p in prefs:
        if n % p == 0:
            return p
    raise ValueError(f"no tile in {prefs} divides {n}")


def kernel(x, meta_tokens, ln0_g, ln0_b, w_in, q_norm_g, kv_norm_g, w_uq, w_ukv, conv_w, attn_out_g, conv_out_g,
           w_o, ln1_g, ln1_b, peer_w_query, peer_sub_keys, peer_u, peer_v, ln2_g, ln2_b):
    b, s, d = x.shape
    assert w_in.shape[0] == 1 and s % _Q_TILE == 0
    conv_dim = conv_w.shape[2]
    h = MLA_HEADS
    row = lambda a: a.reshape(1, -1).astype(F32)

    wi = w_in[0]
    o_kv, o_kr, o_b = Q_RANK, Q_RANK + KV_RANK, Q_RANK + KV_RANK + D_ROPE
    w_kr = wi[:, o_kr:o_b]
    zpad = jnp.zeros((d, LANES - D_ROPE), F32)
    win = jnp.concatenate(
        [wi[:, :o_kr], w_kr, zpad, _swap_halves(w_kr), zpad, wi[:, o_b:]], axis=1).astype(BF16)
    wq3 = w_uq[0].reshape(Q_RANK, h, D_QK)
    zq = jnp.zeros((Q_RANK, h, QK_PAD - D_QK), F32)
    wqat = jnp.concatenate([wq3, zq], axis=2).reshape(Q_RANK, h * QK_PAD).T.astype(BF16)
    wqbt = jnp.concatenate([_swap_halves(wq3[:, :, D_NOPE:]), zq], axis=2).reshape(Q_RANK, h * LANES).T.astype(BF16)
    wkv3 = w_ukv[0].reshape(KV_RANK, h, D_NOPE + D_V)
    wk = wkv3[:, :, :D_NOPE].reshape(KV_RANK, h * D_NOPE).astype(BF16)
    wvt = wkv3[:, :, D_NOPE:].reshape(KV_RANK, h * D_V).T.astype(BF16)
    wo = w_o[0].astype(BF16)
    wpq = peer_w_query[0].astype(BF16)
    sk = peer_sub_keys[0].astype(BF16)
    sx, su, sv, sh = _peer_scales(ln1_g[0], ln1_b[0], peer_u[0], peer_v[0], d)
    ut = (peer_u[0].T * su).astype(F8)
    pv = (peer_v[0] * sv).astype(F8)
    unscale = jnp.stack([1.0 / (sx * su), 1.0 / (sh * sv)]).astype(F32)

    inproj_w = (row(ln0_g), row(ln0_b), win, row(q_norm_g[0]), row(kv_norm_g[0]), wqat, wqbt, wk, wvt,
                conv_w[0].astype(F32), row(conv_out_g[0]))

    t1m, t2m = _rope_tables(0, N_META)
    _, km, vmt, _, utail = _inproj(meta_tokens[None].astype(F32), t1m, t2m, t1m.T, t2m.T,
                                   jnp.zeros((8, conv_dim), F32), *inproj_w, tm=N_META)
    km = jnp.pad(km[0], ((0, 0), (0, LANES - N_META), (0, 0)))
    vmt = jnp.pad(vmt[0, :, 0], ((0, 0), (0, 0), (0, LANES - N_META)))

    t1, t2 = _rope_tables(N_META, s)
    qt, k, vt, convn, _ = _inproj(x, t1, t2, t1.T, t2.T, utail[0], *inproj_w, tm=_KV_BLOCK)
    oa = _attention(qt, k, vt, km, vmt, tq=_Q_TILE)
    h1, h1q, st = _outproj(sx.reshape(1), x, oa, convn, row(ln0_g), row(ln0_b), row(attn_out_g[0]), wo,
                           row(ln1_g[0]), row(ln1_b[0]), wpq, sk, tm=_pick(s, (256, 128)))
    i1, i2, g = _route(st, tl=LANES)
    tok = lambda a: jnp.swapaxes(a, 1, 2).reshape(b * s, a.shape[1])
    gd = _scatter(tok(i1), tok(i2), tok(g) * sh, tg=_pick(b * s, (128, 64)))
    out = _peer(unscale, h1q.reshape(b * s, d), h1.reshape(b * s, d), ut, pv, gd, row(ln2_g[0]), row(ln2_b[0]),
                tm=_pick(b * s, (1024, 512)), ec=1024)
    return out.reshape(b, s, d)
```

```python
import functools

import jax
import jax.numpy as jnp
import numpy as np
from jax import lax
from jax.experimental import pallas as pl
from jax.experimental.pallas import tpu as pltpu

CHUNK = 64
N_META = 16
MLA_HEADS = 8
D_NOPE = 128
D_ROPE = 64
D_QK = D_NOPE + D_ROPE
D_V = 128
Q_RANK = 384
KV_RANK = 512
ROPE_THETA = 10000.0
ATTN_DIM = MLA_HEADS * D_V
CONV_WIDTH = 3
PEER_HEADS = 8
N_KEYS = 128
D_HALF = 128
PEER_TOPK = 16
DEPTH = 1
DEEPNORM_ALPHA = (2.0 * DEPTH) ** 0.25
EPS = 1e-5
NEG_INF = -1e30

_CHUNK_SHIFT = CHUNK.bit_length() - 1
_TOPK_SHIFT = PEER_TOPK.bit_length() - 1
assert 1 << _CHUNK_SHIFT == CHUNK and 1 << _TOPK_SHIFT == PEER_TOPK

LANES = 128
QK_PAD = 256
_KV_BLOCK = 512
_Q_BLOCK = 256
_Q_TILE = 2 * _KV_BLOCK
_DENOM_ROWS = 16
_LOG2E = 1.4426950408889634
VMEM_LIMIT = 58 * 1024 * 1024

F32 = jnp.float32
BF16 = jnp.bfloat16
F8 = jnp.float8_e4m3fn
_F8_TARGET = 224.0


def _dot(a, b):
    return jnp.dot(a, b, preferred_element_type=F32)


def _dot_nt(a, b):
    return lax.dot_general(a, b, (((1,), (1,)), ((), ())), preferred_element_type=F32)


def _layer_norm(x, g, b):
    mu = jnp.mean(x, axis=-1, keepdims=True)
    xc = x - mu
    var = jnp.mean(xc * xc, axis=-1, keepdims=True)
    return xc * lax.rsqrt(var + EPS) * g + b


def _rms_norm(x, g):
    return x * lax.rsqrt(jnp.mean(x * x, axis=-1, keepdims=True) + EPS) * g


def _const_spec(shape):
    nd = len(shape)
    return pl.BlockSpec(shape, lambda *_: (0,) * nd, pipeline_mode=pl.Buffered(1))


_OFF_CQ = 0
_OFF_CKV = _OFF_CQ + Q_RANK
_OFF_KRA = _OFF_CKV + KV_RANK
_OFF_KRB = _OFF_KRA + LANES
_OFF_BG = _OFF_KRB + LANES


def _inproj_kernel(x_ref, t1_ref, t2_ref, t1t_ref, t2t_ref, uinit_ref, ln0g_ref, ln0b_ref, win_ref, qg_ref,
                   kvg_ref, wqat_ref, wqbt_ref, wk_ref, wvt_ref, convw_ref, convg_ref,
                   qt_out, k_out, vt_out, convn_out, utail_out, ubuf, *, tm, vb, conv_dim):
    i = pl.program_id(1)
    off_cg = _OFF_BG + conv_dim
    off_hc = off_cg + conv_dim

    @pl.when(i == 0)
    def _():
        ubuf[0:8, :] = uinit_ref[...]

    hb = _layer_norm(x_ref[...], ln0g_ref[...], ln0b_ref[...]).astype(BF16)
    t1 = t1_ref[...]
    t2 = t2_ref[...]
    t1t = t1t_ref[...]
    t2t = t2t_ref[...]
    qscale = D_QK ** -0.5 * _LOG2E

    cq = _dot(hb, win_ref[:, _OFF_CQ:_OFF_CQ + Q_RANK])
    ckv = _dot(hb, win_ref[:, _OFF_CKV:_OFF_CKV + KV_RANK])
    kra = _dot(hb, win_ref[:, _OFF_KRA:_OFF_KRA + LANES])
    krb = _dot(hb, win_ref[:, _OFF_KRB:_OFF_KRB + LANES])
    krot = (kra * t1 + krb * t2).astype(BF16)
    cqn = _rms_norm(cq, qg_ref[...]).astype(BF16)
    ckvn = _rms_norm(ckv, kvg_ref[...]).astype(BF16)

    for h in range(MLA_HEADS):
        qat = _dot_nt(wqat_ref[h * QK_PAD:(h + 1) * QK_PAD, :], cqn)
        qbt = _dot_nt(wqbt_ref[h * LANES:(h + 1) * LANES, :], cqn)
        qt_out[h, 0:LANES, :] = (qat[0:LANES, :] * qscale).astype(BF16)
        qt_out[h, LANES:QK_PAD, :] = ((qat[LANES:QK_PAD, :] * t1t + qbt * t2t) * qscale).astype(BF16)
        k_out[h, :, 0:LANES] = _dot(ckvn, wk_ref[:, h * LANES:(h + 1) * LANES]).astype(BF16)
        k_out[h, :, LANES:QK_PAD] = krot
        vt = _dot_nt(wvt_ref[h * D_V:(h + 1) * D_V, :], ckvn).astype(BF16)
        for bk in range(tm // vb):
            vt_out[h, bk] = vt[:, bk * vb:(bk + 1) * vb]

    bg = _dot(hb, win_ref[:, _OFF_BG:_OFF_BG + conv_dim])
    cg = _dot(hb, win_ref[:, off_cg:off_cg + conv_dim])
    hc = _dot(hb, win_ref[:, off_hc:off_hc + conv_dim])
    u = cg * hc
    ubuf[8:8 + tm, :] = u
    u1 = ubuf[7:7 + tm, :]
    u2 = ubuf[6:6 + tm, :]
    w = convw_ref[...]
    y = bg * (u2 * w[0:1, :] + u1 * w[1:2, :] + u * w[2:3, :])
    convn_out[...] = _rms_norm(y, convg_ref[...]).astype(BF16)
    tail = ubuf[tm:tm + 8, :]
    utail_out[...] = tail
    ubuf[0:8, :] = tail


def _inproj(x, t1, t2, t1t, t2t, uinit, ln0g, ln0b, win, qg, kvg, wqat, wqbt, wk, wvt, convw, convg, *, tm):
    b, s, d = x.shape
    conv_dim = convw.shape[1]
    grid = (b, s // tm)
    h = MLA_HEADS
    vb = min(tm, _KV_BLOCK)
    out_shape = (
        jax.ShapeDtypeStruct((b, h, QK_PAD, s), BF16),
        jax.ShapeDtypeStruct((b, h, s, QK_PAD), BF16),
        jax.ShapeDtypeStruct((b, h, s // vb, D_V, vb), BF16),
        jax.ShapeDtypeStruct((b, s, conv_dim), BF16),
        jax.ShapeDtypeStruct((b, 8, conv_dim), F32),
    )
    in_specs = [
        pl.BlockSpec((None, tm, d), lambda bi, i: (bi, i, 0)),
        pl.BlockSpec((tm, LANES), lambda bi, i: (i, 0)),
        pl.BlockSpec((tm, LANES), lambda bi, i: (i, 0)),
        pl.BlockSpec((LANES, tm), lambda bi, i: (0, i)),
        pl.BlockSpec((LANES, tm), lambda bi, i: (0, i)),
        _const_spec(uinit.shape), _const_spec(ln0g.shape), _const_spec(ln0b.shape), _const_spec(win.shape),
        _const_spec(qg.shape), _const_spec(kvg.shape), _const_spec(wqat.shape), _const_spec(wqbt.shape),
        _const_spec(wk.shape), _const_spec(wvt.shape), _const_spec(convw.shape), _const_spec(convg.shape),
    ]
    out_specs = (
        pl.BlockSpec((None, h, QK_PAD, tm), lambda bi, i: (bi, 0, 0, i)),
        pl.BlockSpec((None, h, tm, QK_PAD), lambda bi, i: (bi, 0, i, 0)),
        pl.BlockSpec((None, h, tm // vb, D_V, vb), lambda bi, i: (bi, 0, i, 0, 0)),
        pl.BlockSpec((None, tm, conv_dim), lambda bi, i: (bi, i, 0)),
        pl.BlockSpec((None, 8, conv_dim), lambda bi, i: (bi, 0, 0)),
    )
    return pl.pallas_call(
        functools.partial(_inproj_kernel, tm=tm, vb=vb, conv_dim=conv_dim),
        grid=grid, in_specs=in_specs, out_specs=out_specs, out_shape=out_shape,
        scratch_shapes=[pltpu.VMEM((tm + 8, conv_dim), F32)],
        compiler_params=pltpu.CompilerParams(
            dimension_semantics=("arbitrary", "arbitrary"), vmem_limit_bytes=VMEM_LIMIT),
        name="inproj",
    )(x, t1, t2, t1t, t2t, uinit, ln0g, ln0b, win, qg, kvg, wqat, wqbt, wk, wvt, convw, convg)


def _attn_kernel(qt_ref, k_ref, vt_ref, km_ref, vmt_ref, o_ref, acc_ref, sa_ref, sb_ref, *, tq):
    i = pl.program_id(2)
    kb, qb = _KV_BLOCK, _Q_BLOCK
    ncb = tq // qb
    nd = tq // kb
    assert nd == 2, "the two-buffer pipeline consumes key blocks in pairs"

    def cols(cb):
        return slice(cb * qb, (cb + 1) * qb)

    def scores(j, first_cb=0):
        k = k_ref[pl.ds(pl.multiple_of(j * kb, kb), kb), :]
        return _dot(k, qt_ref[:, first_cb * qb:])

    def with_ones(vt):
        return jnp.concatenate([vt, jnp.ones((_DENOM_ROWS, vt.shape[1]), BF16)], axis=0)

    def absorb(cb, m, s, smax, vt1):
        mn = jnp.maximum(m, smax)
        a = jnp.exp2(m - mn)
        p = jnp.exp2(s - mn)
        acc_ref[:, cols(cb)] = a * acc_ref[:, cols(cb)] + _dot(vt1, p.astype(BF16))
        return mn

    def stash(s_ref, k, cb):
        s = _dot(k, qt_ref[:, cols(cb)])
        s_ref[:, cols(cb)] = s
        return jnp.max(s, axis=0, keepdims=True)

    def key_block(j):
        return k_ref[pl.ds(pl.multiple_of(j * kb, kb), kb), :]

    def trade(carry, s_old, smax_old, vt_old, s_new, j_new):
        k_new = key_block(j_new)
        vt1 = with_ones(vt_old)
        ms, smax_new = list(carry), []
        for cb in range(ncb):
            s = s_old[:, cols(cb)]
            smax_new.append(stash(s_new, k_new, cb))
            ms[cb] = absorb(cb, ms[cb], s, smax_old[cb], vt1)
        return tuple(ms), tuple(smax_new)

    k0 = key_block(0)
    max_a = tuple(stash(sa_ref, k0, cb) for cb in range(ncb))

    ms = []
    meta_rows = lax.broadcasted_iota(jnp.int32, (km_ref.shape[0], qb), 0) < N_META
    vmt1 = with_ones(vmt_ref[...])
    for cb in range(ncb):
        s = jnp.where(meta_rows, _dot(km_ref[...], qt_ref[:, cols(cb)]), NEG_INF)
        m0 = jnp.max(s, axis=0, keepdims=True)
        ms.append(m0)
        acc_ref[:, cols(cb)] = _dot(vmt1, jnp.exp2(s - m0).astype(BF16))

    def body(jp, carry):
        stats, max_a = carry
        j = 2 * jp
        stats, max_b = trade(stats, sa_ref, max_a, vt_ref[j], sb_ref, j + 1)
        return trade(stats, sb_ref, max_b, vt_ref[j + 1], sa_ref, j + 2)

    ms, _ = lax.fori_loop(0, i, body, (tuple(ms), max_a))
    ms = list(ms)

    for jj in range(nd):
        j = i * nd + jj
        first_cb = (jj * kb) // qb
        s_all = None if jj == 0 else scores(j, first_cb)
        vt1 = with_ones(vt_ref[j])
        for cb in range(first_cb, ncb):
            s = sa_ref[:, cols(cb)] if jj == 0 else s_all[:, (cb - first_cb) * qb:(cb - first_cb + 1) * qb]
            if (jj + 1) * kb > cb * qb + CHUNK:
                key_chunk = (lax.broadcasted_iota(jnp.int32, (kb, qb), 0) + jj * kb) >> _CHUNK_SHIFT
                qry_chunk = (lax.broadcasted_iota(jnp.int32, (kb, qb), 1) + cb * qb) >> _CHUNK_SHIFT
                s = jnp.where(key_chunk <= qry_chunk, s, NEG_INF)
            ms[cb] = absorb(cb, ms[cb], s, jnp.max(s, axis=0, keepdims=True), vt1)

    for cb in range(ncb):
        o = acc_ref[0:D_V, cols(cb)] / acc_ref[D_V:D_V + 1, cols(cb)]
        o_ref[cb * qb:(cb + 1) * qb, :] = o.T.astype(BF16)


def _attention(qt, k, vt, km, vmt, *, tq):
    b, h, _, s = qt.shape
    grid = (b, h, s // tq)
    nkb = vt.shape[2]
    return pl.pallas_call(
        functools.partial(_attn_kernel, tq=tq),
        grid=grid,
        in_specs=[
            pl.BlockSpec((None, None, QK_PAD, tq), lambda bi, hi, i: (bi, hi, 0, i)),
            pl.BlockSpec((None, None, s, QK_PAD), lambda bi, hi, i: (bi, hi, 0, 0)),
            pl.BlockSpec((None, None, nkb, D_V, _KV_BLOCK), lambda bi, hi, i: (bi, hi, 0, 0, 0)),
            pl.BlockSpec((None, LANES, QK_PAD), lambda bi, hi, i: (hi, 0, 0)),
            pl.BlockSpec((None, D_V, LANES), lambda bi, hi, i: (hi, 0, 0)),
        ],
        out_specs=pl.BlockSpec((None, tq, D_V), lambda bi, hi, i: (bi, i, hi)),
        out_shape=jax.ShapeDtypeStruct((b, s, h * D_V), BF16),
        scratch_shapes=[pltpu.VMEM((D_V + _DENOM_ROWS, tq), F32), pltpu.VMEM((_KV_BLOCK, tq), F32),
                        pltpu.VMEM((_KV_BLOCK, tq), F32)],
        compiler_params=pltpu.CompilerParams(
            dimension_semantics=("arbitrary", "arbitrary", "arbitrary"), vmem_limit_bytes=VMEM_LIMIT),
        name="attn",
    )(qt, k, vt, km, vmt)


def _outproj_kernel(xscale_ref, x_ref, oa_ref, cn_ref, ln0g_ref, ln0b_ref, ag_ref, wo_ref, ln1g_ref, ln1b_ref,
                    wq_ref, sk_ref, h1_out, h1q_out, st_out):
    h0 = _layer_norm(x_ref[...], ln0g_ref[...], ln0b_ref[...])
    an = _rms_norm(oa_ref[...].astype(F32), ag_ref[...]).astype(BF16)
    mixed = _dot(an, wo_ref[0:ATTN_DIM, :]) + _dot(cn_ref[...], wo_ref[ATTN_DIM:, :])
    h1 = _layer_norm(DEEPNORM_ALPHA * h0 + mixed, ln1g_ref[...], ln1b_ref[...])
    h1_out[...] = h1
    h1q_out[...] = (h1 * xscale_ref[0]).astype(F8)
    h1b = h1.astype(BF16)
    pq = _dot(h1b, wq_ref[...])
    for hc in range(PEER_HEADS * 2):
        pqs = pq[:, hc * D_HALF:(hc + 1) * D_HALF].astype(BF16)
        st_out[hc * N_KEYS:(hc + 1) * N_KEYS, :] = _dot_nt(sk_ref[hc % 2], pqs)


def _outproj(xscale, x, oa, cn, ln0g, ln0b, ag, wo, ln1g, ln1b, wq, sk, *, tm):
    b, s, d = x.shape
    grid = (b, s // tm)
    nsc = PEER_HEADS * 2 * N_KEYS
    tok = lambda w: pl.BlockSpec((None, tm, w), lambda bi, i: (bi, i, 0))
    return pl.pallas_call(
        _outproj_kernel,
        grid=grid,
        in_specs=[pl.BlockSpec(memory_space=pltpu.SMEM), tok(d), tok(oa.shape[2]), tok(cn.shape[2]),
                  _const_spec(ln0g.shape), _const_spec(ln0b.shape), _const_spec(ag.shape), _const_spec(wo.shape),
                  _const_spec(ln1g.shape), _const_spec(ln1b.shape), _const_spec(wq.shape), _const_spec(sk.shape)],
        out_specs=(tok(d), tok(d), pl.BlockSpec((None, nsc, tm), lambda bi, i: (bi, 0, i))),
        out_shape=(jax.ShapeDtypeStruct((b, s, d), F32), jax.ShapeDtypeStruct((b, s, d), F8),
                   jax.ShapeDtypeStruct((b, nsc, s), F32)),
        compiler_params=pltpu.CompilerParams(
            dimension_semantics=("arbitrary", "arbitrary"), vmem_limit_bytes=VMEM_LIMIT),
        name="outproj",
    )(xscale, x, oa, cn, ln0g, ln0b, ag, wo, ln1g, ln1b, wq, sk)


_BIG_ID = 1 << 20


_SUBLANES = 8


def _top_keys(s_ref, base, n):
    w = s_ref.shape[1]
    nv = N_KEYS // _SUBLANES
    assert n <= nv
    sub = lax.broadcasted_iota(jnp.int32, (_SUBLANES, w), 0)
    vals = [s_ref[pl.ds(pl.multiple_of(base + r * _SUBLANES, _SUBLANES), _SUBLANES), :] for r in range(nv)]
    ids = [sub + r * _SUBLANES for r in range(nv)]
    for rnd in range(nv):
        for i in range(rnd % 2, nv - 1, 2):
            swap = vals[i + 1] > vals[i]
            hi, lo = jnp.maximum(vals[i], vals[i + 1]), jnp.minimum(vals[i], vals[i + 1])
            ids[i], ids[i + 1] = jnp.where(swap, ids[i + 1], ids[i]), jnp.where(swap, ids[i], ids[i + 1])
            vals[i], vals[i + 1] = hi, lo
    rows = lax.broadcasted_iota(jnp.int32, (n, w), 0)
    out_v = jnp.zeros((n, w), F32)
    out_i = jnp.zeros((n, w), jnp.int32)
    val_rows = []
    for t in range(n):
        m = jnp.max(vals[0], axis=0, keepdims=True)
        pick = jnp.min(jnp.where(vals[0] == m, ids[0], _BIG_ID), axis=0, keepdims=True)
        out_v = jnp.where(rows == t, m, out_v)
        out_i = jnp.where(rows == t, pick, out_i)
        val_rows.append(m)
        win = ids[0] == pick
        for r in range(n - 1 - t):
            vals[r] = jnp.where(win, vals[r + 1], vals[r])
            ids[r] = jnp.where(win, ids[r + 1], ids[r])
    return out_v, out_i, val_rows


def _top_pair_sums(v1, v2rows, n):
    w = v1.shape[1]
    half = n // 2
    assert half == _SUBLANES
    ra = lax.broadcasted_iota(jnp.int32, (half, w), 0)
    top = v1[0:half, :]
    bot = v1[half:, :] + v2rows[0]
    bot_id = (ra + half) * n
    vals = [jnp.where(ra < n // (b + 1), top + v2rows[b], -jnp.inf) for b in range(n)]
    ids = [ra * n + b for b in range(n)]
    rows = lax.broadcasted_iota(jnp.int32, (n, w), 0)
    out_v = jnp.zeros((n, w), F32)
    out_i = jnp.zeros((n, w), jnp.int32)
    for t in range(n):
        m = jnp.max(jnp.maximum(vals[0], bot), axis=0, keepdims=True)
        lowest = jnp.minimum(jnp.where(vals[0] == m, ids[0], _BIG_ID), jnp.where(bot == m, bot_id, _BIG_ID))
        pick = jnp.min(lowest, axis=0, keepdims=True)
        out_v = jnp.where(rows == t, m, out_v)
        out_i = jnp.where(rows == t, pick, out_i)
        bot = jnp.where(bot_id == pick, -jnp.inf, bot)
        win = ids[0] == pick
        for b in range(n - 1 - t):
            vals[b] = jnp.where(win, vals[b + 1], vals[b])
            ids[b] = jnp.where(win, ids[b + 1], ids[b])
    return out_v, out_i


def _route_kernel(st_ref, i1_out, i2_out, g_out):
    w = st_ref.shape[1]
    k = PEER_TOPK

    def head(h, _):
        base = pl.multiple_of(h * 2 * N_KEYS, 2 * N_KEYS)
        v1, i1, _ = _top_keys(st_ref, base, k)
        _, i2, v2rows = _top_keys(st_ref, base + N_KEYS, k)
        best, best_id = _top_pair_sums(v1, v2rows, k)
        a_sel = best_id >> _TOPK_SHIFT
        b_sel = best_id & (k - 1)
        e1 = jnp.zeros((k, w), jnp.int32)
        e2 = jnp.zeros((k, w), jnp.int32)
        for c in range(k):
            e1 = jnp.where(a_sel == c, i1[c:c + 1, :], e1)
            e2 = jnp.where(b_sel == c, i2[c:c + 1, :], e2)
        ex = jnp.exp(best - best[0:1, :])
        gate = ex / jnp.sum(ex, axis=0, keepdims=True)
        o = pl.multiple_of(h * k, k)
        i1_out[pl.ds(o, k), :] = e1
        i2_out[pl.ds(o, k), :] = e2
        g_out[pl.ds(o, k), :] = gate
        return 0

    lax.fori_loop(0, PEER_HEADS, head, 0, unroll=4)


def _route(st, *, tl):
    b, nsc, s = st.shape
    nj = PEER_HEADS * PEER_TOPK
    grid = (b, s // tl)
    ospec = pl.BlockSpec((None, nj, tl), lambda bi, i: (bi, 0, i))
    return pl.pallas_call(
        _route_kernel,
        grid=grid,
        in_specs=[pl.BlockSpec((None, nsc, tl), lambda bi, i: (bi, 0, i))],
        out_specs=(ospec, ospec, ospec),
        out_shape=(jax.ShapeDtypeStruct((b, nj, s), jnp.int32), jax.ShapeDtypeStruct((b, nj, s), jnp.int32),
                   jax.ShapeDtypeStruct((b, nj, s), F32)),
        compiler_params=pltpu.CompilerParams(dimension_semantics=("arbitrary", "arbitrary")),
        name="route",
    )(st)


_GROUP = 16
_PITCH = N_KEYS + 4


def _scatter_kernel(i1_ref, i2_ref, g_ref, gd_out, stage_a, stage_b):
    tg = i1_ref.shape[0]
    nj = i1_ref.shape[1]
    sub = lax.broadcasted_iota(jnp.int32, (N_KEYS, nj), 0)
    stages = (stage_a, stage_b)

    def fill(grp):
        stage = stages[grp % 2]
        for t in range(_GROUP):
            tok = grp * _GROUP + t
            r1 = i1_ref[tok:tok + 1, :]
            r2 = i2_ref[tok:tok + 1, :]
            gg = g_ref[tok:tok + 1, :]
            p1 = jnp.where(sub == r1, gg, 0.0).astype(BF16)
            p2 = jnp.where(sub == r2, 1.0, 0.0).astype(BF16)
            stage[t * _PITCH:t * _PITCH + N_KEYS, :] = _dot_nt(p1, p2)

    def drain(grp):
        stage = stages[grp % 2]
        for a in range(N_KEYS):
            rows = stage[pl.ds(a, _GROUP, stride=_PITCH), :]
            gd_out[grp * _GROUP:(grp + 1) * _GROUP, a * N_KEYS:(a + 1) * N_KEYS] = rows.astype(BF16)

    ngroups = tg // _GROUP
    fill(0)
    for grp in range(ngroups):
        if grp + 1 < ngroups:
            fill(grp + 1)
        drain(grp)


def _scatter(i1, i2, g, *, tg):
    t, nj = i1.shape
    ne = N_KEYS * N_KEYS
    spec = pl.BlockSpec((tg, nj), lambda i: (i, 0))
    return pl.pallas_call(
        _scatter_kernel,
        grid=(t // tg,),
        in_specs=[spec, spec, spec],
        out_specs=pl.BlockSpec((tg, ne), lambda i: (i, 0)),
        out_shape=jax.ShapeDtypeStruct((t, ne), BF16),
        scratch_shapes=[pltpu.VMEM((_GROUP * _PITCH, N_KEYS), F32), pltpu.VMEM((_GROUP * _PITCH, N_KEYS), F32)],
        compiler_params=pltpu.CompilerParams(dimension_semantics=("arbitrary",)),
        name="scatter",
    )(i1, i2, g)


def _peer_kernel(unscale_ref, xq_ref, h1_ref, ut_ref, v_ref, gd_ref, ln2g_ref, ln2b_ref, o_ref):
    e = pl.program_id(1)

    @pl.when(e == 0)
    def _():
        o_ref[...] = jnp.zeros_like(o_ref)

    raw = _dot(xq_ref[...], ut_ref[...])
    w = 1.0 + lax.erf(raw * (unscale_ref[0] * 2.0 ** -0.5))
    hd = (raw * w * gd_ref[...].astype(F32)).astype(F8)
    o_ref[...] += _dot(hd, v_ref[...])

    @pl.when(e == pl.num_programs(1) - 1)
    def _():
        f = o_ref[...] * unscale_ref[1]
        o_ref[...] = _layer_norm(DEEPNORM_ALPHA * h1_ref[...] + f, ln2g_ref[...], ln2b_ref[...])


def _peer(unscale, xq, h1, ut, v, gd, ln2g, ln2b, *, tm, ec):
    t, d = xq.shape
    ne = v.shape[0]
    grid = (t // tm, ne // ec)
    return pl.pallas_call(
        _peer_kernel,
        grid=grid,
        in_specs=[
            pl.BlockSpec(memory_space=pltpu.SMEM),
            pl.BlockSpec((tm, d), lambda i, e: (i, 0)),
            pl.BlockSpec((tm, d), lambda i, e: (i, 0), pipeline_mode=pl.Buffered(1)),
            pl.BlockSpec((d, ec), lambda i, e: (0, e)),
            pl.BlockSpec((ec, d), lambda i, e: (e, 0)),
            pl.BlockSpec((tm, ec), lambda i, e: (i, e)),
            _const_spec(ln2g.shape), _const_spec(ln2b.shape),
        ],
        out_specs=pl.BlockSpec((tm, d), lambda i, e: (i, 0)),
        out_shape=jax.ShapeDtypeStruct((t, d), F32),
        compiler_params=pltpu.CompilerParams(
            dimension_semantics=("arbitrary", "arbitrary"), vmem_limit_bytes=VMEM_LIMIT),
        name="peer",
    )(unscale, xq, h1, ut, v, gd, ln2g, ln2b)


def _pow2_below(x):
    return jnp.exp2(jnp.floor(jnp.log2(x)))


def _peer_scales(ln1_g, ln1_b, peer_u, peer_v, d):
    tiny = jnp.float32(1e-30)
    h_elem = jnp.max(jnp.sqrt(float(d)) * jnp.abs(ln1_g) + jnp.abs(ln1_b))
    h_norm = jnp.sqrt(float(d)) * jnp.max(jnp.abs(ln1_g)) + jnp.sqrt(jnp.sum(ln1_b * ln1_b))
    u_row = jnp.sqrt(jnp.max(jnp.sum(peer_u * peer_u, axis=1)))
    sx = _pow2_below(_F8_TARGET / jnp.maximum(h_elem, tiny))
    su = _pow2_below(_F8_TARGET / jnp.maximum(jnp.max(jnp.abs(peer_u)), tiny))
    sv = _pow2_below(_F8_TARGET / jnp.maximum(jnp.max(jnp.abs(peer_v)), tiny))
    sh = _pow2_below(_F8_TARGET / jnp.maximum(h_norm * u_row, tiny))
    return sx, su, sv, sh


def _rope_tables(first_pos, n):
    inv = 1.0 / (ROPE_THETA ** (jnp.arange(0, D_ROPE, 2, dtype=F32) / D_ROPE))
    ang = (first_pos + jnp.arange(n, dtype=F32))[:, None] * inv[None, :]
    cos, sin = jnp.cos(ang), jnp.sin(ang)
    z = jnp.zeros((n, LANES - D_ROPE), F32)
    return jnp.concatenate([cos, cos, z], axis=1), jnp.concatenate([-sin, sin, z], axis=1)


def _swap_halves(w):
    half = w.shape[-1] // 2
    return jnp.concatenate([w[..., half:], w[..., :half]], axis=-1)


def _pick(n, prefs):
    for p in prefs:
        if n % p == 0:
            return p
    raise ValueError(f"no tile in {prefs} divides {n}")


def kernel(x, meta_tokens, ln0_g, ln0_b, w_in, q_norm_g, kv_norm_g, w_uq, w_ukv, conv_w, attn_out_g, conv_out_g,
           w_o, ln1_g, ln1_b, peer_w_query, peer_sub_keys, peer_u, peer_v, ln2_g, ln2_b):
    b, s, d = x.shape
    assert w_in.shape[0] == 1 and s % _Q_TILE == 0
    conv_dim = conv_w.shape[2]
    h = MLA_HEADS
    row = lambda a: a.reshape(1, -1).astype(F32)

    wi = w_in[0]
    o_kv, o_kr, o_b = Q_RANK, Q_RANK + KV_RANK, Q_RANK + KV_RANK + D_ROPE
    w_kr = wi[:, o_kr:o_b]
    zpad = jnp.zeros((d, LANES - D_ROPE), F32)
    win = jnp.concatenate(
        [wi[:, :o_kr], w_kr, zpad, _swap_halves(w_kr), zpad, wi[:, o_b:]], axis=1).astype(BF16)
    wq3 = w_uq[0].reshape(Q_RANK, h, D_QK)
    zq = jnp.zeros((Q_RANK, h, QK_PAD - D_QK), F32)
    wqat = jnp.concatenate([wq3, zq], axis=2).reshape(Q_RANK, h * QK_PAD).T.astype(BF16)
    wqbt = jnp.concatenate([_swap_halves(wq3[:, :, D_NOPE:]), zq], axis=2).reshape(Q_RANK, h * LANES).T.astype(BF16)
    wkv3 = w_ukv[0].reshape(KV_RANK, h, D_NOPE + D_V)
    wk = wkv3[:, :, :D_NOPE].reshape(KV_RANK, h * D_NOPE).astype(BF16)
    wvt = wkv3[:, :, D_NOPE:].reshape(KV_RANK, h * D_V).T.astype(BF16)
    wo = w_o[0].astype(BF16)
    wpq = peer_w_query[0].astype(BF16)
    sk = peer_sub_keys[0].astype(BF16)
    sx, su, sv, sh = _peer_scales(ln1_g[0], ln1_b[0], peer_u[0], peer_v[0], d)
    ut = (peer_u[0].T * su).astype(F8)
    pv = (peer_v[0] * sv).astype(F8)
    unscale = jnp.stack([1.0 / (sx * su), 1.0 / (sh * sv)]).astype(F32)

    inproj_w = (row(ln0_g), row(ln0_b), win, row(q_norm_g[0]), row(kv_norm_g[0]), wqat, wqbt, wk, wvt,
                conv_w[0].astype(F32), row(conv_out_g[0]))

    t1m, t2m = _rope_tables(0, N_META)
    _, km, vmt, _, utail = _inproj(meta_tokens[None].astype(F32), t1m, t2m, t1m.T, t2m.T,
                                   jnp.zeros((8, conv_dim), F32), *inproj_w, tm=N_META)
    km = jnp.pad(km[0], ((0, 0), (0, LANES - N_META), (0, 0)))
    vmt = jnp.pad(vmt[0, :, 0], ((0, 0), (0, 0), (0, LANES - N_META)))

    t1, t2 = _rope_tables(N_META, s)
    qt, k, vt, convn, _ = _inproj(x, t1, t2, t1.T, t2.T, utail[0], *inproj_w, tm=_KV_BLOCK)
    oa = _attention(qt, k, vt, km, vmt, tq=_Q_TILE)
    h1, h1q, st = _outproj(sx.reshape(1), x, oa, convn, row(ln0_g), row(ln0_b), row(attn_out_g[0]), wo,
                           row(ln1_g[0]), row(ln1_b[0]), wpq, sk, tm=_pick(s, (256, 128)))
    i1, i2, g = _route(st, tl=LANES)
    tok = lambda a: jnp.swapaxes(a, 1, 2).reshape(b * s, a.shape[1])
    gd = _scatter(tok(i1), tok(i2), tok(g) * (sh * 0.5 * unscale[0]), tg=_pick(b * s, (128, 64)))
    out = _peer(unscale, h1q.reshape(b * s, d), h1.reshape(b * s, d), ut, pv, gd, row(ln2_g[0]), row(ln2_b[0]),
                tm=_pick(b * s, (1024, 512)), ec=1024)
    return out.reshape(b, s, d)
```

```python
import functools

import jax
import jax.numpy as jnp
import numpy as np
from jax import lax
from jax.experimental import pallas as pl
from jax.experimental.pallas import tpu as pltpu

CHUNK = 64
N_META = 16
MLA_HEADS = 8
D_NOPE = 128
D_ROPE = 64
D_QK = D_NOPE + D_ROPE
D_V = 128
Q_RANK = 384
KV_RANK = 512
ROPE_THETA = 10000.0
ATTN_DIM = MLA_HEADS * D_V
CONV_WIDTH = 3
PEER_HEADS = 8
N_KEYS = 128
D_HALF = 128
PEER_TOPK = 16
DEPTH = 1
DEEPNORM_ALPHA = (2.0 * DEPTH) ** 0.25
EPS = 1e-5
NEG_INF = -1e30

_CHUNK_SHIFT = CHUNK.bit_length() - 1
_TOPK_SHIFT = PEER_TOPK.bit_length() - 1
assert 1 << _CHUNK_SHIFT == CHUNK and 1 << _TOPK_SHIFT == PEER_TOPK

LANES = 128
QK_PAD = 256
_KV_BLOCK = 512
_Q_BLOCK = 256
_Q_TILE = 2 * _KV_BLOCK
_DENOM_ROWS = 16
_LOG2E = 1.4426950408889634
VMEM_LIMIT = 58 * 1024 * 1024

F32 = jnp.float32
BF16 = jnp.bfloat16
F8 = jnp.float8_e4m3fn
_F8_TARGET = 224.0


def _dot(a, b):
    return jnp.dot(a, b, preferred_element_type=F32)


def _dot_nt(a, b):
    return lax.dot_general(a, b, (((1,), (1,)), ((), ())), preferred_element_type=F32)


def _layer_norm(x, g, b):
    mu = jnp.mean(x, axis=-1, keepdims=True)
    xc = x - mu
    var = jnp.mean(xc * xc, axis=-1, keepdims=True)
    return xc * lax.rsqrt(var + EPS) * g + b


def _rms_norm(x, g):
    return x * lax.rsqrt(jnp.mean(x * x, axis=-1, keepdims=True) + EPS) * g


def _const_spec(shape):
    nd = len(shape)
    return pl.BlockSpec(shape, lambda *_: (0,) * nd, pipeline_mode=pl.Buffered(1))


_OFF_CQ = 0
_OFF_CKV = _OFF_CQ + Q_RANK
_OFF_KRA = _OFF_CKV + KV_RANK
_OFF_KRB = _OFF_KRA + LANES
_OFF_BG = _OFF_KRB + LANES


def _inproj_kernel(x_ref, t1_ref, t2_ref, t1t_ref, t2t_ref, uinit_ref, ln0g_ref, ln0b_ref, win_ref, qg_ref,
                   kvg_ref, wqat_ref, wqbt_ref, wk_ref, wvt_ref, convw_ref, convg_ref,
                   qt_out, k_out, vt_out, convn_out, utail_out, ubuf, *, tm, vb, conv_dim):
    i = pl.program_id(1)
    off_cg = _OFF_BG + conv_dim
    off_hc = off_cg + conv_dim

    @pl.when(i == 0)
    def _():
        ubuf[0:8, :] = uinit_ref[...]

    hb = _layer_norm(x_ref[...], ln0g_ref[...], ln0b_ref[...]).astype(BF16)
    t1 = t1_ref[...]
    t2 = t2_ref[...]
    t1t = t1t_ref[...]
    t2t = t2t_ref[...]
    qscale = D_QK ** -0.5 * _LOG2E

    cq = _dot(hb, win_ref[:, _OFF_CQ:_OFF_CQ + Q_RANK])
    ckv = _dot(hb, win_ref[:, _OFF_CKV:_OFF_CKV + KV_RANK])
    kra = _dot(hb, win_ref[:, _OFF_KRA:_OFF_KRA + LANES])
    krb = _dot(hb, win_ref[:, _OFF_KRB:_OFF_KRB + LANES])
    krot = (kra * t1 + krb * t2).astype(BF16)
    cqn = _rms_norm(cq, qg_ref[...]).astype(BF16)
    ckvn = _rms_norm(ckv, kvg_ref[...]).astype(BF16)

    for h in range(MLA_HEADS):
        qat = _dot_nt(wqat_ref[h * QK_PAD:(h + 1) * QK_PAD, :], cqn)
        qbt = _dot_nt(wqbt_ref[h * LANES:(h + 1) * LANES, :], cqn)
        qt_out[h, 0:LANES, :] = (qat[0:LANES, :] * qscale).astype(BF16)
        qt_out[h, LANES:QK_PAD, :] = ((qat[LANES:QK_PAD, :] * t1t + qbt * t2t) * qscale).astype(BF16)
        k_out[h, :, 0:LANES] = _dot(ckvn, wk_ref[:, h * LANES:(h + 1) * LANES]).astype(BF16)
        k_out[h, :, LANES:QK_PAD] = krot
        vt = _dot_nt(wvt_ref[h * D_V:(h + 1) * D_V, :], ckvn).astype(BF16)
        for bk in range(tm // vb):
            vt_out[h, bk] = vt[:, bk * vb:(bk + 1) * vb]

    bg = _dot(hb, win_ref[:, _OFF_BG:_OFF_BG + conv_dim])
    cg = _dot(hb, win_ref[:, off_cg:off_cg + conv_dim])
    hc = _dot(hb, win_ref[:, off_hc:off_hc + conv_dim])
    u = cg * hc
    ubuf[8:8 + tm, :] = u
    u1 = ubuf[7:7 + tm, :]
    u2 = ubuf[6:6 + tm, :]
    w = convw_ref[...]
    y = bg * (u2 * w[0:1, :] + u1 * w[1:2, :] + u * w[2:3, :])
    convn_out[...] = _rms_norm(y, convg_ref[...]).astype(BF16)
    tail = ubuf[tm:tm + 8, :]
    utail_out[...] = tail
    ubuf[0:8, :] = tail


def _inproj(x, t1, t2, t1t, t2t, uinit, ln0g, ln0b, win, qg, kvg, wqat, wqbt, wk, wvt, convw, convg, *, tm):
    b, s, d = x.shape
    conv_dim = convw.shape[1]
    grid = (b, s // tm)
    h = MLA_HEADS
    vb = min(tm, _KV_BLOCK)
    out_shape = (
        jax.ShapeDtypeStruct((b, h, QK_PAD, s), BF16),
        jax.ShapeDtypeStruct((b, h, s, QK_PAD), BF16),
        jax.ShapeDtypeStruct((b, h, s // vb, D_V, vb), BF16),
        jax.ShapeDtypeStruct((b, s, conv_dim), BF16),
        jax.ShapeDtypeStruct((b, 8, conv_dim), F32),
    )
    in_specs = [
        pl.BlockSpec((None, tm, d), lambda bi, i: (bi, i, 0)),
        pl.BlockSpec((tm, LANES), lambda bi, i: (i, 0)),
        pl.BlockSpec((tm, LANES), lambda bi, i: (i, 0)),
        pl.BlockSpec((LANES, tm), lambda bi, i: (0, i)),
        pl.BlockSpec((LANES, tm), lambda bi, i: (0, i)),
        _const_spec(uinit.shape), _const_spec(ln0g.shape), _const_spec(ln0b.shape), _const_spec(win.shape),
        _const_spec(qg.shape), _const_spec(kvg.shape), _const_spec(wqat.shape), _const_spec(wqbt.shape),
        _const_spec(wk.shape), _const_spec(wvt.shape), _const_spec(convw.shape), _const_spec(convg.shape),
    ]
    out_specs = (
        pl.BlockSpec((None, h, QK_PAD, tm), lambda bi, i: (bi, 0, 0, i)),
        pl.BlockSpec((None, h, tm, QK_PAD), lambda bi, i: (bi, 0, i, 0)),
        pl.BlockSpec((None, h, tm // vb, D_V, vb), lambda bi, i: (bi, 0, i, 0, 0)),
        pl.BlockSpec((None, tm, conv_dim), lambda bi, i: (bi, i, 0)),
        pl.BlockSpec((None, 8, conv_dim), lambda bi, i: (bi, 0, 0)),
    )
    return pl.pallas_call(
        functools.partial(_inproj_kernel, tm=tm, vb=vb, conv_dim=conv_dim),
        grid=grid, in_specs=in_specs, out_specs=out_specs, out_shape=out_shape,
        scratch_shapes=[pltpu.VMEM((tm + 8, conv_dim), F32)],
        compiler_params=pltpu.CompilerParams(
            dimension_semantics=("arbitrary", "arbitrary"), vmem_limit_bytes=VMEM_LIMIT),
        name="inproj",
    )(x, t1, t2, t1t, t2t, uinit, ln0g, ln0b, win, qg, kvg, wqat, wqbt, wk, wvt, convw, convg)


def _attn_kernel(qt_ref, k_ref, vt_ref, km_ref, vmt_ref, o_ref, acc_ref, sa_ref, sb_ref, *, tq):
    i = pl.program_id(2)
    kb, qb = _KV_BLOCK, _Q_BLOCK
    ncb = tq // qb
    nd = tq // kb
    assert nd == 2, "the two-buffer pipeline consumes key blocks in pairs"

    def cols(cb):
        return slice(cb * qb, (cb + 1) * qb)

    def with_ones(vt):
        return jnp.concatenate([vt, jnp.ones((_DENOM_ROWS, vt.shape[1]), BF16)], axis=0)

    def absorb(cb, m, s, smax, vt1):
        mn = jnp.maximum(m, smax)
        a = jnp.exp2(m - mn)
        p = jnp.exp2(s - mn)
        acc_ref[:, cols(cb)] = a * acc_ref[:, cols(cb)] + _dot(vt1, p.astype(BF16))
        return mn

    def stash(s_ref, k, cb):
        s = _dot(k, qt_ref[:, cols(cb)])
        s_ref[:, cols(cb)] = s
        return jnp.max(s, axis=0, keepdims=True)

    def key_block(j):
        return k_ref[pl.ds(pl.multiple_of(j * kb, kb), kb), :]

    def trade(carry, s_old, smax_old, vt_old, s_new, j_new):
        k_new = key_block(j_new)
        vt1 = with_ones(vt_old)
        ms, smax_new = list(carry), []
        for cb in range(ncb):
            s = s_old[:, cols(cb)]
            smax_new.append(stash(s_new, k_new, cb))
            ms[cb] = absorb(cb, ms[cb], s, smax_old[cb], vt1)
        return tuple(ms), tuple(smax_new)

    k0 = key_block(0)
    max_a = tuple(stash(sa_ref, k0, cb) for cb in range(ncb))

    ms = []
    meta_rows = lax.broadcasted_iota(jnp.int32, (km_ref.shape[0], qb), 0) < N_META
    vmt1 = with_ones(vmt_ref[...])
    for cb in range(ncb):
        s = jnp.where(meta_rows, _dot(km_ref[...], qt_ref[:, cols(cb)]), NEG_INF)
        m0 = jnp.max(s, axis=0, keepdims=True)
        ms.append(m0)
        acc_ref[:, cols(cb)] = _dot(vmt1, jnp.exp2(s - m0).astype(BF16))

    def pair(j, carry):
        stats, max_a = carry
        stats, max_b = trade(stats, sa_ref, max_a, vt_ref[j], sb_ref, j + 1)
        return trade(stats, sb_ref, max_b, vt_ref[j + 1], sa_ref, j + 2)

    carry = lax.fori_loop(0, i // 2, lambda t, c: pair(4 * t + 2, pair(4 * t, c)), (tuple(ms), max_a))
    ms, _ = lax.fori_loop(0, i % 2, lambda t, c: pair(2 * (i - 1), c), carry)
    ms = list(ms)

    def masked(s, jj, cb):
        if (jj + 1) * kb <= cb * qb + CHUNK:
            return s
        key_chunk = (lax.broadcasted_iota(jnp.int32, (kb, qb), 0) + jj * kb) >> _CHUNK_SHIFT
        qry_chunk = (lax.broadcasted_iota(jnp.int32, (kb, qb), 1) + cb * qb) >> _CHUNK_SHIFT
        return jnp.where(key_chunk <= qry_chunk, s, NEG_INF)

    j0 = i * nd
    k_last = key_block(j0 + 1)
    second_from = kb // qb
    vt1 = with_ones(vt_ref[j0])
    for cb in range(ncb):
        s = masked(sa_ref[:, cols(cb)], 0, cb)
        if cb >= second_from:
            sb_ref[:, cols(cb)] = _dot(k_last, qt_ref[:, cols(cb)])
        ms[cb] = absorb(cb, ms[cb], s, jnp.max(s, axis=0, keepdims=True), vt1)
    vt1 = with_ones(vt_ref[j0 + 1])
    for cb in range(second_from, ncb):
        s = masked(sb_ref[:, cols(cb)], 1, cb)
        ms[cb] = absorb(cb, ms[cb], s, jnp.max(s, axis=0, keepdims=True), vt1)

    for cb in range(ncb):
        o = acc_ref[0:D_V, cols(cb)] / acc_ref[D_V:D_V + 1, cols(cb)]
        o_ref[cb * qb:(cb + 1) * qb, :] = o.T.astype(BF16)


def _attention(qt, k, vt, km, vmt, *, tq):
    b, h, _, s = qt.shape
    grid = (b, h, s // tq)
    nkb = vt.shape[2]
    return pl.pallas_call(
        functools.partial(_attn_kernel, tq=tq),
        grid=grid,
        in_specs=[
            pl.BlockSpec((None, None, QK_PAD, tq), lambda bi, hi, i: (bi, hi, 0, i)),
            pl.BlockSpec((None, None, s, QK_PAD), lambda bi, hi, i: (bi, hi, 0, 0)),
            pl.BlockSpec((None, None, nkb, D_V, _KV_BLOCK), lambda bi, hi, i: (bi, hi, 0, 0, 0)),
            pl.BlockSpec((None, LANES, QK_PAD), lambda bi, hi, i: (hi, 0, 0)),
            pl.BlockSpec((None, D_V, LANES), lambda bi, hi, i: (hi, 0, 0)),
        ],
        out_specs=pl.BlockSpec((None, tq, D_V), lambda bi, hi, i: (bi, i, hi)),
        out_shape=jax.ShapeDtypeStruct((b, s, h * D_V), BF16),
        scratch_shapes=[pltpu.VMEM((D_V + _DENOM_ROWS, tq), F32), pltpu.VMEM((_KV_BLOCK, tq), F32),
                        pltpu.VMEM((_KV_BLOCK, tq), F32)],
        compiler_params=pltpu.CompilerParams(
            dimension_semantics=("arbitrary", "arbitrary", "arbitrary"), vmem_limit_bytes=VMEM_LIMIT),
        name="attn",
    )(qt, k, vt, km, vmt)


def _outproj_kernel(xscale_ref, x_ref, oa_ref, cn_ref, ln0g_ref, ln0b_ref, ag_ref, wo_ref, ln1g_ref, ln1b_ref,
                    wq_ref, sk_ref, h1_out, h1q_out, st_out):
    h0 = _layer_norm(x_ref[...], ln0g_ref[...], ln0b_ref[...])
    an = _rms_norm(oa_ref[...].astype(F32), ag_ref[...]).astype(BF16)
    mixed = _dot(an, wo_ref[0:ATTN_DIM, :]) + _dot(cn_ref[...], wo_ref[ATTN_DIM:, :])
    h1 = _layer_norm(DEEPNORM_ALPHA * h0 + mixed, ln1g_ref[...], ln1b_ref[...])
    h1_out[...] = h1
    h1q_out[...] = (h1 * xscale_ref[0]).astype(F8)
    h1b = h1.astype(BF16)
    pq = _dot(h1b, wq_ref[...])
    for hc in range(PEER_HEADS * 2):
        pqs = pq[:, hc * D_HALF:(hc + 1) * D_HALF].astype(BF16)
        st_out[hc * N_KEYS:(hc + 1) * N_KEYS, :] = _dot_nt(sk_ref[hc % 2], pqs)


def _outproj(xscale, x, oa, cn, ln0g, ln0b, ag, wo, ln1g, ln1b, wq, sk, *, tm):
    b, s, d = x.shape
    grid = (b, s // tm)
    nsc = PEER_HEADS * 2 * N_KEYS
    tok = lambda w: pl.BlockSpec((None, tm, w), lambda bi, i: (bi, i, 0))
    return pl.pallas_call(
        _outproj_kernel,
        grid=grid,
        in_specs=[pl.BlockSpec(memory_space=pltpu.SMEM), tok(d), tok(oa.shape[2]), tok(cn.shape[2]),
                  _const_spec(ln0g.shape), _const_spec(ln0b.shape), _const_spec(ag.shape), _const_spec(wo.shape),
                  _const_spec(ln1g.shape), _const_spec(ln1b.shape), _const_spec(wq.shape), _const_spec(sk.shape)],
        out_specs=(tok(d), tok(d), pl.BlockSpec((None, nsc, tm), lambda bi, i: (bi, 0, i))),
        out_shape=(jax.ShapeDtypeStruct((b, s, d), F32), jax.ShapeDtypeStruct((b, s, d), F8),
                   jax.ShapeDtypeStruct((b, nsc, s), F32)),
        compiler_params=pltpu.CompilerParams(
            dimension_semantics=("arbitrary", "arbitrary"), vmem_limit_bytes=VMEM_LIMIT),
        name="outproj",
    )(xscale, x, oa, cn, ln0g, ln0b, ag, wo, ln1g, ln1b, wq, sk)


_BIG_ID = 1 << 20


_SUBLANES = 8


def _top_keys(s_ref, base, n):
    w = s_ref.shape[1]
    nv = N_KEYS // _SUBLANES
    assert n <= nv
    sub = lax.broadcasted_iota(jnp.int32, (_SUBLANES, w), 0)
    vals = [s_ref[pl.ds(pl.multiple_of(base + r * _SUBLANES, _SUBLANES), _SUBLANES), :] for r in range(nv)]
    ids = [sub + r * _SUBLANES for r in range(nv)]
    for rnd in range(nv):
        for i in range(rnd % 2, nv - 1, 2):
            swap = vals[i + 1] > vals[i]
            hi, lo = jnp.maximum(vals[i], vals[i + 1]), jnp.minimum(vals[i], vals[i + 1])
            ids[i], ids[i + 1] = jnp.where(swap, ids[i + 1], ids[i]), jnp.where(swap, ids[i], ids[i + 1])
            vals[i], vals[i + 1] = hi, lo
    rows = lax.broadcasted_iota(jnp.int32, (n, w), 0)
    out_v = jnp.zeros((n, w), F32)
    out_i = jnp.zeros((n, w), jnp.int32)
    val_rows = []
    for t in range(n):
        m = jnp.max(vals[0], axis=0, keepdims=True)
        pick = jnp.min(jnp.where(vals[0] == m, ids[0], _BIG_ID), axis=0, keepdims=True)
        out_v = jnp.where(rows == t, m, out_v)
        out_i = jnp.where(rows == t, pick, out_i)
        val_rows.append(m)
        win = ids[0] == pick
        for r in range(n - 1 - t):
            vals[r] = jnp.where(win, vals[r + 1], vals[r])
            ids[r] = jnp.where(win, ids[r + 1], ids[r])
    return out_v, out_i, val_rows


def _top_pair_sums(v1, v2rows, n):
    w = v1.shape[1]
    half = n // 2
    assert half == _SUBLANES
    ra = lax.broadcasted_iota(jnp.int32, (half, w), 0)
    top = v1[0:half, :]
    bot = v1[half:, :] + v2rows[0]
    bot_id = (ra + half) * n
    vals = [jnp.where(ra < n // (b + 1), top + v2rows[b], -jnp.inf) for b in range(n)]
    ids = [ra * n + b for b in range(n)]
    rows = lax.broadcasted_iota(jnp.int32, (n, w), 0)
    out_v = jnp.zeros((n, w), F32)
    out_i = jnp.zeros((n, w), jnp.int32)
    for t in range(n):
        m = jnp.max(jnp.maximum(vals[0], bot), axis=0, keepdims=True)
        lowest = jnp.minimum(jnp.where(vals[0] == m, ids[0], _BIG_ID), jnp.where(bot == m, bot_id, _BIG_ID))
        pick = jnp.min(lowest, axis=0, keepdims=True)
        out_v = jnp.where(rows == t, m, out_v)
        out_i = jnp.where(rows == t, pick, out_i)
        bot = jnp.where(bot_id == pick, -jnp.inf, bot)
        win = ids[0] == pick
        for b in range(n - 1 - t):
            vals[b] = jnp.where(win, vals[b + 1], vals[b])
            ids[b] = jnp.where(win, ids[b + 1], ids[b])
    return out_v, out_i


def _route_kernel(st_ref, i1_out, i2_out, g_out):
    w = st_ref.shape[1]
    k = PEER_TOPK

    def head(h, _):
        base = pl.multiple_of(h * 2 * N_KEYS, 2 * N_KEYS)
        v1, i1, _ = _top_keys(st_ref, base, k)
        _, i2, v2rows = _top_keys(st_ref, base + N_KEYS, k)
        best, best_id = _top_pair_sums(v1, v2rows, k)
        a_sel = best_id >> _TOPK_SHIFT
        b_sel = best_id & (k - 1)
        e1 = jnp.zeros((k, w), jnp.int32)
        e2 = jnp.zeros((k, w), jnp.int32)
        for c in range(k):
            e1 = jnp.where(a_sel == c, i1[c:c + 1, :], e1)
            e2 = jnp.where(b_sel == c, i2[c:c + 1, :], e2)
        ex = jnp.exp(best - best[0:1, :])
        gate = ex / jnp.sum(ex, axis=0, keepdims=True)
        o = pl.multiple_of(h * k, k)
        i1_out[pl.ds(o, k), :] = e1
        i2_out[pl.ds(o, k), :] = e2
        g_out[pl.ds(o, k), :] = gate
        return 0

    lax.fori_loop(0, PEER_HEADS, head, 0, unroll=4)


def _route(st, *, tl):
    b, nsc, s = st.shape
    nj = PEER_HEADS * PEER_TOPK
    grid = (b, s // tl)
    ospec = pl.BlockSpec((None, nj, tl), lambda bi, i: (bi, 0, i))
    return pl.pallas_call(
        _route_kernel,
        grid=grid,
        in_specs=[pl.BlockSpec((None, nsc, tl), lambda bi, i: (bi, 0, i))],
        out_specs=(ospec, ospec, ospec),
        out_shape=(jax.ShapeDtypeStruct((b, nj, s), jnp.int32), jax.ShapeDtypeStruct((b, nj, s), jnp.int32),
                   jax.ShapeDtypeStruct((b, nj, s), F32)),
        compiler_params=pltpu.CompilerParams(dimension_semantics=("arbitrary", "arbitrary")),
        name="route",
    )(st)


_GROUP = 16
_PITCH = N_KEYS + 4


def _scatter_kernel(i1_ref, i2_ref, g_ref, gd_out, stage_a, stage_b):
    tg = i1_ref.shape[0]
    nj = i1_ref.shape[1]
    sub = lax.broadcasted_iota(jnp.int32, (N_KEYS, nj), 0)
    stages = (stage_a, stage_b)

    def fill(grp):
        stage = stages[grp % 2]
        for t in range(_GROUP):
            tok = grp * _GROUP + t
            r1 = i1_ref[tok:tok + 1, :]
            r2 = i2_ref[tok:tok + 1, :]
            gg = g_ref[tok:tok + 1, :]
            p1 = jnp.where(sub == r1, gg, 0.0).astype(BF16)
            p2 = jnp.where(sub == r2, 1.0, 0.0).astype(BF16)
            stage[t * _PITCH:t * _PITCH + N_KEYS, :] = _dot_nt(p1, p2)

    def drain(grp):
        stage = stages[grp % 2]
        for a in range(N_KEYS):
            rows = stage[pl.ds(a, _GROUP, stride=_PITCH), :]
            gd_out[grp * _GROUP:(grp + 1) * _GROUP, a * N_KEYS:(a + 1) * N_KEYS] = rows.astype(BF16)

    ngroups = tg // _GROUP
    fill(0)
    for grp in range(ngroups):
        if grp + 1 < ngroups:
            fill(grp + 1)
        drain(grp)


def _scatter(i1, i2, g, *, tg):
    t, nj = i1.shape
    ne = N_KEYS * N_KEYS
    spec = pl.BlockSpec((tg, nj), lambda i: (i, 0))
    return pl.pallas_call(
        _scatter_kernel,
        grid=(t // tg,),
        in_specs=[spec, spec, spec],
        out_specs=pl.BlockSpec((tg, ne), lambda i: (i, 0)),
        out_shape=jax.ShapeDtypeStruct((t, ne), BF16),
        scratch_shapes=[pltpu.VMEM((_GROUP * _PITCH, N_KEYS), F32), pltpu.VMEM((_GROUP * _PITCH, N_KEYS), F32)],
        compiler_params=pltpu.CompilerParams(dimension_semantics=("arbitrary",)),
        name="scatter",
    )(i1, i2, g)


def _peer_kernel(unscale_ref, xq_ref, h1_ref, ut_ref, v_ref, gd_ref, ln2g_ref, ln2b_ref, o_ref):
    e = pl.program_id(1)

    @pl.when(e == 0)
    def _():
        o_ref[...] = jnp.zeros_like(o_ref)

    raw = _dot(xq_ref[...], ut_ref[...])
    w = 1.0 + lax.erf(raw * (unscale_ref[0] * 2.0 ** -0.5))
    hd = (raw * w * gd_ref[...].astype(F32)).astype(F8)
    o_ref[...] += _dot(hd, v_ref[...])

    @pl.when(e == pl.num_programs(1) - 1)
    def _():
        f = o_ref[...] * unscale_ref[1]
        o_ref[...] = _layer_norm(DEEPNORM_ALPHA * h1_ref[...] + f, ln2g_ref[...], ln2b_ref[...])


def _peer(unscale, xq, h1, ut, v, gd, ln2g, ln2b, *, tm, ec):
    t, d = xq.shape
    ne = v.shape[0]
    grid = (t // tm, ne // ec)
    return pl.pallas_call(
        _peer_kernel,
        grid=grid,
        in_specs=[
            pl.BlockSpec(memory_space=pltpu.SMEM),
            pl.BlockSpec((tm, d), lambda i, e: (i, 0)),
            pl.BlockSpec((tm, d), lambda i, e: (i, 0), pipeline_mode=pl.Buffered(1)),
            pl.BlockSpec((d, ec), lambda i, e: (0, e)),
            pl.BlockSpec((ec, d), lambda i, e: (e, 0)),
            pl.BlockSpec((tm, ec), lambda i, e: (i, e)),
            _const_spec(ln2g.shape), _const_spec(ln2b.shape),
        ],
        out_specs=pl.BlockSpec((tm, d), lambda i, e: (i, 0)),
        out_shape=jax.ShapeDtypeStruct((t, d), F32),
        compiler_params=pltpu.CompilerParams(
            dimension_semantics=("arbitrary", "arbitrary"), vmem_limit_bytes=VMEM_LIMIT),
        name="peer",
    )(unscale, xq, h1, ut, v, gd, ln2g, ln2b)


def _pow2_below(x):
    return jnp.exp2(jnp.floor(jnp.log2(x)))


def _peer_scales(ln1_g, ln1_b, peer_u, peer_v, d):
    tiny = jnp.float32(1e-30)
    h_elem = jnp.max(jnp.sqrt(float(d)) * jnp.abs(ln1_g) + jnp.abs(ln1_b))
    h_norm = jnp.sqrt(float(d)) * jnp.max(jnp.abs(ln1_g)) + jnp.sqrt(jnp.sum(ln1_b * ln1_b))
    u_row = jnp.sqrt(jnp.max(jnp.sum(peer_u * peer_u, axis=1)))
    sx = _pow2_below(_F8_TARGET / jnp.maximum(h_elem, tiny))
    su = _pow2_below(_F8_TARGET / jnp.maximum(jnp.max(jnp.abs(peer_u)), tiny))
    sv = _pow2_below(_F8_TARGET / jnp.maximum(jnp.max(jnp.abs(peer_v)), tiny))
    sh = _pow2_below(_F8_TARGET / jnp.maximum(h_norm * u_row, tiny))
    return sx, su, sv, sh


def _rope_tables(first_pos, n):
    inv = 1.0 / (ROPE_THETA ** (jnp.arange(0, D_ROPE, 2, dtype=F32) / D_ROPE))
    ang = (first_pos + jnp.arange(n, dtype=F32))[:, None] * inv[None, :]
    cos, sin = jnp.cos(ang), jnp.sin(ang)
    z = jnp.zeros((n, LANES - D_ROPE), F32)
    return jnp.concatenate([cos, cos, z], axis=1), jnp.concatenate([-sin, sin, z], axis=1)


def _swap_halves(w):
    half = w.shape[-1] // 2
    return jnp.concatenate([w[..., half:], w[..., :half]], axis=-1)


def _pick(n, prefs):
    for p in prefs:
        if n % p == 0:
            return p
    raise ValueError(f"no tile in {prefs} divides {n}")


def kernel(x, meta_tokens, ln0_g, ln0_b, w_in, q_norm_g, kv_norm_g, w_uq, w_ukv, conv_w, attn_out_g, conv_out_g,
           w_o, ln1_g, ln1_b, peer_w_query, peer_sub_keys, peer_u, peer_v, ln2_g, ln2_b):
    b, s, d = x.shape
    assert w_in.shape[0] == 1 and s % _Q_TILE == 0
    conv_dim = conv_w.shape[2]
    h = MLA_HEADS
    row = lambda a: a.reshape(1, -1).astype(F32)

    wi = w_in[0]
    o_kv, o_kr, o_b = Q_RANK, Q_RANK + KV_RANK, Q_RANK + KV_RANK + D_ROPE
    w_kr = wi[:, o_kr:o_b]
    zpad = jnp.zeros((d, LANES - D_ROPE), F32)
    win = jnp.concatenate(
        [wi[:, :o_kr], w_kr, zpad, _swap_halves(w_kr), zpad, wi[:, o_b:]], axis=1).astype(BF16)
    wq3 = w_uq[0].reshape(Q_RANK, h, D_QK)
    zq = jnp.zeros((Q_RANK, h, QK_PAD - D_QK), F32)
    wqat = jnp.concatenate([wq3, zq], axis=2).reshape(Q_RANK, h * QK_PAD).T.astype(BF16)
    wqbt = jnp.concatenate([_swap_halves(wq3[:, :, D_NOPE:]), zq], axis=2).reshape(Q_RANK, h * LANES).T.astype(BF16)
    wkv3 = w_ukv[0].reshape(KV_RANK, h, D_NOPE + D_V)
    wk = wkv3[:, :, :D_NOPE].reshape(KV_RANK, h * D_NOPE).astype(BF16)
    wvt = wkv3[:, :, D_NOPE:].reshape(KV_RANK, h * D_V).T.astype(BF16)
    wo = w_o[0].astype(BF16)
    wpq = peer_w_query[0].astype(BF16)
    sk = peer_sub_keys[0].astype(BF16)
    sx, su, sv, sh = _peer_scales(ln1_g[0], ln1_b[0], peer_u[0], peer_v[0], d)
    ut = (peer_u[0].T * su).astype(F8)
    pv = (peer_v[0] * sv).astype(F8)
    unscale = jnp.stack([1.0 / (sx * su), 1.0 / (sh * sv)]).astype(F32)

    inproj_w = (row(ln0_g), row(ln0_b), win, row(q_norm_g[0]), row(kv_norm_g[0]), wqat, wqbt, wk, wvt,
                conv_w[0].astype(F32), row(conv_out_g[0]))

    t1m, t2m = _rope_tables(0, N_META)
    _, km, vmt, _, utail = _inproj(meta_tokens[None].astype(F32), t1m, t2m, t1m.T, t2m.T,
                                   jnp.zeros((8, conv_dim), F32), *inproj_w, tm=N_META)
    km = jnp.pad(km[0], ((0, 0), (0, LANES - N_META), (0, 0)))
    vmt = jnp.pad(vmt[0, :, 0], ((0, 0), (0, 0), (0, LANES - N_META)))

    t1, t2 = _rope_tables(N_META, s)
    qt, k, vt, convn, _ = _inproj(x, t1, t2, t1.T, t2.T, utail[0], *inproj_w, tm=_KV_BLOCK)
    oa = _attention(qt, k, vt, km, vmt, tq=_Q_TILE)
    h1, h1q, st = _outproj(sx.reshape(1), x, oa, convn, row(ln0_g), row(ln0_b), row(attn_out_g[0]), wo,
                           row(ln1_g[0]), row(ln1_b[0]), wpq, sk, tm=_pick(s, (256, 128)))
    i1, i2, g = _route(st, tl=LANES)
    tok = lambda a: jnp.swapaxes(a, 1, 2).reshape(b * s, a.shape[1])
    gd = _scatter(tok(i1), tok(i2), tok(g) * (sh * 0.5 * unscale[0]), tg=_pick(b * s, (128, 64)))
    out = _peer(unscale, h1q.reshape(b * s, d), h1.reshape(b * s, d), ut, pv, gd, row(ln2_g[0]), row(ln2_b[0]),
                tm=_pick(b * s, (1024, 512)), ec=1024)
    return out.reshape(b, s, d)
```

```python
import functools

import jax
import jax.numpy as jnp
import numpy as np
from jax import lax
from jax.experimental import pallas as pl
from jax.experimental.pallas import tpu as pltpu

CHUNK = 64
N_META = 16
MLA_HEADS = 8
D_NOPE = 128
D_ROPE = 64
D_QK = D_NOPE + D_ROPE
D_V = 128
Q_RANK = 384
KV_RANK = 512
ROPE_THETA = 10000.0
ATTN_DIM = MLA_HEADS * D_V
CONV_WIDTH = 3
PEER_HEADS = 8
N_KEYS = 128
D_HALF = 128
PEER_TOPK = 16
DEPTH = 1
DEEPNORM_ALPHA = (2.0 * DEPTH) ** 0.25
EPS = 1e-5
NEG_INF = -1e30

_CHUNK_SHIFT = CHUNK.bit_length() - 1
_TOPK_SHIFT = PEER_TOPK.bit_length() - 1
assert 1 << _CHUNK_SHIFT == CHUNK and 1 << _TOPK_SHIFT == PEER_TOPK

LANES = 128
QK_PAD = 256
_KV_BLOCK = 512
_Q_BLOCK = 256
_Q_TILE = 2 * _KV_BLOCK
_PAIRS_PER_TRIP = 2
_DENOM_ROWS = 16
_LOG2E = 1.4426950408889634
VMEM_LIMIT = 58 * 1024 * 1024

F32 = jnp.float32
BF16 = jnp.bfloat16
F8 = jnp.float8_e4m3fn
_F8_TARGET = 224.0


def _dot(a, b):
    return jnp.dot(a, b, preferred_element_type=F32)


def _dot_nt(a, b):
    return lax.dot_general(a, b, (((1,), (1,)), ((), ())), preferred_element_type=F32)


def _layer_norm(x, g, b):
    mu = jnp.mean(x, axis=-1, keepdims=True)
    xc = x - mu
    var = jnp.mean(xc * xc, axis=-1, keepdims=True)
    return xc * lax.rsqrt(var + EPS) * g + b


def _rms_norm(x, g):
    return x * lax.rsqrt(jnp.mean(x * x, axis=-1, keepdims=True) + EPS) * g


def _const_spec(shape):
    nd = len(shape)
    return pl.BlockSpec(shape, lambda *_: (0,) * nd, pipeline_mode=pl.Buffered(1))


_OFF_CQ = 0
_OFF_CKV = _OFF_CQ + Q_RANK
_OFF_KRA = _OFF_CKV + KV_RANK
_OFF_KRB = _OFF_KRA + LANES
_OFF_BG = _OFF_KRB + LANES


def _inproj_kernel(x_ref, t1_ref, t2_ref, t1t_ref, t2t_ref, uinit_ref, ln0g_ref, ln0b_ref, win_ref, qg_ref,
                   kvg_ref, wqat_ref, wqbt_ref, wk_ref, wvt_ref, convw_ref, convg_ref,
                   qt_out, k_out, vt_out, convn_out, utail_out, ubuf, *, tm, vb, conv_dim):
    i = pl.program_id(1)
    off_cg = _OFF_BG + conv_dim
    off_hc = off_cg + conv_dim

    @pl.when(i == 0)
    def _():
        ubuf[0:8, :] = uinit_ref[...]

    hb = _layer_norm(x_ref[...], ln0g_ref[...], ln0b_ref[...]).astype(BF16)
    t1 = t1_ref[...]
    t2 = t2_ref[...]
    t1t = t1t_ref[...]
    t2t = t2t_ref[...]
    qscale = D_QK ** -0.5 * _LOG2E

    cq = _dot(hb, win_ref[:, _OFF_CQ:_OFF_CQ + Q_RANK])
    ckv = _dot(hb, win_ref[:, _OFF_CKV:_OFF_CKV + KV_RANK])
    kra = _dot(hb, win_ref[:, _OFF_KRA:_OFF_KRA + LANES])
    krb = _dot(hb, win_ref[:, _OFF_KRB:_OFF_KRB + LANES])
    krot = (kra * t1 + krb * t2).astype(BF16)
    cqn = _rms_norm(cq, qg_ref[...]).astype(BF16)
    ckvn = _rms_norm(ckv, kvg_ref[...]).astype(BF16)

    for h in range(MLA_HEADS):
        qat = _dot_nt(wqat_ref[h * QK_PAD:(h + 1) * QK_PAD, :], cqn)
        qbt = _dot_nt(wqbt_ref[h * LANES:(h + 1) * LANES, :], cqn)
        qt_out[h, 0:LANES, :] = (qat[0:LANES, :] * qscale).astype(BF16)
        qt_out[h, LANES:QK_PAD, :] = ((qat[LANES:QK_PAD, :] * t1t + qbt * t2t) * qscale).astype(BF16)
        k_out[h, :, 0:LANES] = _dot(ckvn, wk_ref[:, h * LANES:(h + 1) * LANES]).astype(BF16)
        k_out[h, :, LANES:QK_PAD] = krot
        vt = _dot_nt(wvt_ref[h * D_V:(h + 1) * D_V, :], ckvn).astype(BF16)
        for bk in range(tm // vb):
            vt_out[h, bk] = vt[:, bk * vb:(bk + 1) * vb]

    bg = _dot(hb, win_ref[:, _OFF_BG:_OFF_BG + conv_dim])
    cg = _dot(hb, win_ref[:, off_cg:off_cg + conv_dim])
    hc = _dot(hb, win_ref[:, off_hc:off_hc + conv_dim])
    u = cg * hc
    ubuf[8:8 + tm, :] = u
    u1 = ubuf[7:7 + tm, :]
    u2 = ubuf[6:6 + tm, :]
    w = convw_ref[...]
    y = bg * (u2 * w[0:1, :] + u1 * w[1:2, :] + u * w[2:3, :])
    convn_out[...] = _rms_norm(y, convg_ref[...]).astype(BF16)
    tail = ubuf[tm:tm + 8, :]
    utail_out[...] = tail
    ubuf[0:8, :] = tail


def _inproj(x, t1, t2, t1t, t2t, uinit, ln0g, ln0b, win, qg, kvg, wqat, wqbt, wk, wvt, convw, convg, *, tm):
    b, s, d = x.shape
    conv_dim = convw.shape[1]
    grid = (b, s // tm)
    h = MLA_HEADS
    vb = min(tm, _KV_BLOCK)
    out_shape = (
        jax.ShapeDtypeStruct((b, h, QK_PAD, s), BF16),
        jax.ShapeDtypeStruct((b, h, s, QK_PAD), BF16),
        jax.ShapeDtypeStruct((b, h, s // vb, D_V, vb), BF16),
        jax.ShapeDtypeStruct((b, s, conv_dim), BF16),
        jax.ShapeDtypeStruct((b, 8, conv_dim), F32),
    )
    in_specs = [
        pl.BlockSpec((None, tm, d), lambda bi, i: (bi, i, 0)),
        pl.BlockSpec((tm, LANES), lambda bi, i: (i, 0)),
        pl.BlockSpec((tm, LANES), lambda bi, i: (i, 0)),
        pl.BlockSpec((LANES, tm), lambda bi, i: (0, i)),
        pl.BlockSpec((LANES, tm), lambda bi, i: (0, i)),
        _const_spec(uinit.shape), _const_spec(ln0g.shape), _const_spec(ln0b.shape), _const_spec(win.shape),
        _const_spec(qg.shape), _const_spec(kvg.shape), _const_spec(wqat.shape), _const_spec(wqbt.shape),
        _const_spec(wk.shape), _const_spec(wvt.shape), _const_spec(convw.shape), _const_spec(convg.shape),
    ]
    out_specs = (
        pl.BlockSpec((None, h, QK_PAD, tm), lambda bi, i: (bi, 0, 0, i)),
        pl.BlockSpec((None, h, tm, QK_PAD), lambda bi, i: (bi, 0, i, 0)),
        pl.BlockSpec((None, h, tm // vb, D_V, vb), lambda bi, i: (bi, 0, i, 0, 0)),
        pl.BlockSpec((None, tm, conv_dim), lambda bi, i: (bi, i, 0)),
        pl.BlockSpec((None, 8, conv_dim), lambda bi, i: (bi, 0, 0)),
    )
    return pl.pallas_call(
        functools.partial(_inproj_kernel, tm=tm, vb=vb, conv_dim=conv_dim),
        grid=grid, in_specs=in_specs, out_specs=out_specs, out_shape=out_shape,
        scratch_shapes=[pltpu.VMEM((tm + 8, conv_dim), F32)],
        compiler_params=pltpu.CompilerParams(
            dimension_semantics=("arbitrary", "arbitrary"), vmem_limit_bytes=VMEM_LIMIT),
        name="inproj",
    )(x, t1, t2, t1t, t2t, uinit, ln0g, ln0b, win, qg, kvg, wqat, wqbt, wk, wvt, convw, convg)


def _attn_kernel(qt_ref, k_ref, vt_ref, km_ref, vmt_ref, o_ref, acc_ref, sa_ref, sb_ref, *, tq):
    i = pl.program_id(2)
    kb, qb = _KV_BLOCK, _Q_BLOCK
    ncb = tq // qb
    nd = tq // kb
    assert nd == 2, "the two-buffer pipeline consumes key blocks in pairs"

    def cols(cb):
        return slice(cb * qb, (cb + 1) * qb)

    def with_ones(vt):
        return jnp.concatenate([vt, jnp.ones((_DENOM_ROWS, vt.shape[1]), BF16)], axis=0)

    def absorb(cb, m, s, smax, vt1):
        mn = jnp.maximum(m, smax)
        a = jnp.exp2(m - mn)
        p = jnp.exp2(s - mn)
        acc_ref[:, cols(cb)] = a * acc_ref[:, cols(cb)] + _dot(vt1, p.astype(BF16))
        return mn

    def stash(s_ref, k, cb):
        s = _dot(k, qt_ref[:, cols(cb)])
        s_ref[:, cols(cb)] = s
        return jnp.max(s, axis=0, keepdims=True)

    def key_block(j):
        return k_ref[pl.ds(pl.multiple_of(j * kb, kb), kb), :]

    def trade(carry, s_old, smax_old, vt_old, s_new, j_new):
        k_new = key_block(j_new)
        vt1 = with_ones(vt_old)
        ms, smax_new = list(carry), []
        for cb in range(ncb):
            s = s_old[:, cols(cb)]
            smax_new.append(stash(s_new, k_new, cb))
            ms[cb] = absorb(cb, ms[cb], s, smax_old[cb], vt1)
        return tuple(ms), tuple(smax_new)

    k0 = key_block(0)
    max_a = tuple(stash(sa_ref, k0, cb) for cb in range(ncb))

    ms = []
    meta_rows = lax.broadcasted_iota(jnp.int32, (km_ref.shape[0], qb), 0) < N_META
    vmt1 = with_ones(vmt_ref[...])
    for cb in range(ncb):
        s = jnp.where(meta_rows, _dot(km_ref[...], qt_ref[:, cols(cb)]), NEG_INF)
        m0 = jnp.max(s, axis=0, keepdims=True)
        ms.append(m0)
        acc_ref[:, cols(cb)] = _dot(vmt1, jnp.exp2(s - m0).astype(BF16))

    def pair(j, carry):
        stats, max_a = carry
        stats, max_b = trade(stats, sa_ref, max_a, vt_ref[j], sb_ref, j + 1)
        return trade(stats, sb_ref, max_b, vt_ref[j + 1], sa_ref, j + 2)

    def pairs(t, c):
        for u in range(_PAIRS_PER_TRIP):
            c = pair(2 * (_PAIRS_PER_TRIP * t + u), c)
        return c

    whole = i // _PAIRS_PER_TRIP
    carry = lax.fori_loop(0, whole, pairs, (tuple(ms), max_a))
    ms, _ = lax.fori_loop(whole * _PAIRS_PER_TRIP, i, lambda jp, c: pair(2 * jp, c), carry)
    ms = list(ms)

    def masked(s, jj, cb):
        if (jj + 1) * kb <= cb * qb + CHUNK:
            return s
        key_chunk = (lax.broadcasted_iota(jnp.int32, (kb, qb), 0) + jj * kb) >> _CHUNK_SHIFT
        qry_chunk = (lax.broadcasted_iota(jnp.int32, (kb, qb), 1) + cb * qb) >> _CHUNK_SHIFT
        return jnp.where(key_chunk <= qry_chunk, s, NEG_INF)

    j0 = i * nd
    k_last = key_block(j0 + 1)
    second_from = kb // qb
    vt1 = with_ones(vt_ref[j0])
    for cb in range(ncb):
        s = masked(sa_ref[:, cols(cb)], 0, cb)
        if cb >= second_from:
            sb_ref[:, cols(cb)] = _dot(k_last, qt_ref[:, cols(cb)])
        ms[cb] = absorb(cb, ms[cb], s, jnp.max(s, axis=0, keepdims=True), vt1)
    vt1 = with_ones(vt_ref[j0 + 1])
    for cb in range(second_from, ncb):
        s = masked(sb_ref[:, cols(cb)], 1, cb)
        ms[cb] = absorb(cb, ms[cb], s, jnp.max(s, axis=0, keepdims=True), vt1)

    for cb in range(ncb):
        o = acc_ref[0:D_V, cols(cb)] / acc_ref[D_V:D_V + 1, cols(cb)]
        o_ref[cb * qb:(cb + 1) * qb, :] = o.T.astype(BF16)


def _attention(qt, k, vt, km, vmt, *, tq):
    b, h, _, s = qt.shape
    grid = (b, h, s // tq)
    nkb = vt.shape[2]
    return pl.pallas_call(
        functools.partial(_attn_kernel, tq=tq),
        grid=grid,
        in_specs=[
            pl.BlockSpec((None, None, QK_PAD, tq), lambda bi, hi, i: (bi, hi, 0, i)),
            pl.BlockSpec((None, None, s, QK_PAD), lambda bi, hi, i: (bi, hi, 0, 0)),
            pl.BlockSpec((None, None, nkb, D_V, _KV_BLOCK), lambda bi, hi, i: (bi, hi, 0, 0, 0)),
            pl.BlockSpec((None, LANES, QK_PAD), lambda bi, hi, i: (hi, 0, 0)),
            pl.BlockSpec((None, D_V, LANES), lambda bi, hi, i: (hi, 0, 0)),
        ],
        out_specs=pl.BlockSpec((None, tq, D_V), lambda bi, hi, i: (bi, i, hi)),
        out_shape=jax.ShapeDtypeStruct((b, s, h * D_V), BF16),
        scratch_shapes=[pltpu.VMEM((D_V + _DENOM_ROWS, tq), F32), pltpu.VMEM((_KV_BLOCK, tq), F32),
                        pltpu.VMEM((_KV_BLOCK, tq), F32)],
        compiler_params=pltpu.CompilerParams(
            dimension_semantics=("arbitrary", "arbitrary", "arbitrary"), vmem_limit_bytes=VMEM_LIMIT),
        name="attn",
    )(qt, k, vt, km, vmt)


def _outproj_kernel(xscale_ref, x_ref, oa_ref, cn_ref, ln0g_ref, ln0b_ref, ag_ref, wo_ref, ln1g_ref, ln1b_ref,
                    wq_ref, sk_ref, h1_out, h1q_out, st_out):
    h0 = _layer_norm(x_ref[...], ln0g_ref[...], ln0b_ref[...])
    an = _rms_norm(oa_ref[...].astype(F32), ag_ref[...]).astype(BF16)
    mixed = _dot(an, wo_ref[0:ATTN_DIM, :]) + _dot(cn_ref[...], wo_ref[ATTN_DIM:, :])
    h1 = _layer_norm(DEEPNORM_ALPHA * h0 + mixed, ln1g_ref[...], ln1b_ref[...])
    h1_out[...] = h1
    h1q_out[...] = (h1 * xscale_ref[0]).astype(F8)
    h1b = h1.astype(BF16)
    pq = _dot(h1b, wq_ref[...])
    for hc in range(PEER_HEADS * 2):
        pqs = pq[:, hc * D_HALF:(hc + 1) * D_HALF].astype(BF16)
        st_out[hc * N_KEYS:(hc + 1) * N_KEYS, :] = _dot_nt(sk_ref[hc % 2], pqs)


def _outproj(xscale, x, oa, cn, ln0g, ln0b, ag, wo, ln1g, ln1b, wq, sk, *, tm):
    b, s, d = x.shape
    grid = (b, s // tm)
    nsc = PEER_HEADS * 2 * N_KEYS
    tok = lambda w: pl.BlockSpec((None, tm, w), lambda bi, i: (bi, i, 0))
    return pl.pallas_call(
        _outproj_kernel,
        grid=grid,
        in_specs=[pl.BlockSpec(memory_space=pltpu.SMEM), tok(d), tok(oa.shape[2]), tok(cn.shape[2]),
                  _const_spec(ln0g.shape), _const_spec(ln0b.shape), _const_spec(ag.shape), _const_spec(wo.shape),
                  _const_spec(ln1g.shape), _const_spec(ln1b.shape), _const_spec(wq.shape), _const_spec(sk.shape)],
        out_specs=(tok(d), tok(d), pl.BlockSpec((None, nsc, tm), lambda bi, i: (bi, 0, i))),
        out_shape=(jax.ShapeDtypeStruct((b, s, d), F32), jax.ShapeDtypeStruct((b, s, d), F8),
                   jax.ShapeDtypeStruct((b, nsc, s), F32)),
        compiler_params=pltpu.CompilerParams(
            dimension_semantics=("arbitrary", "arbitrary"), vmem_limit_bytes=VMEM_LIMIT),
        name="outproj",
    )(xscale, x, oa, cn, ln0g, ln0b, ag, wo, ln1g, ln1b, wq, sk)


_BIG_ID = 1 << 20


_SUBLANES = 8


def _top_keys(s_ref, base, n):
    w = s_ref.shape[1]
    nv = N_KEYS // _SUBLANES
    assert n <= nv
    sub = lax.broadcasted_iota(jnp.int32, (_SUBLANES, w), 0)
    vals = [s_ref[pl.ds(pl.multiple_of(base + r * _SUBLANES, _SUBLANES), _SUBLANES), :] for r in range(nv)]
    ids = [sub + r * _SUBLANES for r in range(nv)]
    for rnd in range(nv):
        for i in range(rnd % 2, nv - 1, 2):
            swap = vals[i + 1] > vals[i]
            hi, lo = jnp.maximum(vals[i], vals[i + 1]), jnp.minimum(vals[i], vals[i + 1])
            ids[i], ids[i + 1] = jnp.where(swap, ids[i + 1], ids[i]), jnp.where(swap, ids[i], ids[i + 1])
            vals[i], vals[i + 1] = hi, lo
    rows = lax.broadcasted_iota(jnp.int32, (n, w), 0)
    out_v = jnp.zeros((n, w), F32)
    out_i = jnp.zeros((n, w), jnp.int32)
    val_rows = []
    for t in range(n):
        m = jnp.max(vals[0], axis=0, keepdims=True)
        pick = jnp.min(jnp.where(vals[0] == m, ids[0], _BIG_ID), axis=0, keepdims=True)
        out_v = jnp.where(rows == t, m, out_v)
        out_i = jnp.where(rows == t, pick, out_i)
        val_rows.append(m)
        win = ids[0] == pick
        for r in range(n - 1 - t):
            vals[r] = jnp.where(win, vals[r + 1], vals[r])
            ids[r] = jnp.where(win, ids[r + 1], ids[r])
    return out_v, out_i, val_rows


def _top_pair_sums(v1, v2rows, n):
    w = v1.shape[1]
    half = n // 2
    assert half == _SUBLANES
    ra = lax.broadcasted_iota(jnp.int32, (half, w), 0)
    top = v1[0:half, :]
    bot = v1[half:, :] + v2rows[0]
    bot_id = (ra + half) * n
    vals = [jnp.where(ra < n // (b + 1), top + v2rows[b], -jnp.inf) for b in range(n)]
    ids = [ra * n + b for b in range(n)]
    rows = lax.broadcasted_iota(jnp.int32, (n, w), 0)
    out_v = jnp.zeros((n, w), F32)
    out_i = jnp.zeros((n, w), jnp.int32)
    for t in range(n):
        m = jnp.max(jnp.maximum(vals[0], bot), axis=0, keepdims=True)
        lowest = jnp.minimum(jnp.where(vals[0] == m, ids[0], _BIG_ID), jnp.where(bot == m, bot_id, _BIG_ID))
        pick = jnp.min(lowest, axis=0, keepdims=True)
        out_v = jnp.where(rows == t, m, out_v)
        out_i = jnp.where(rows == t, pick, out_i)
        bot = jnp.where(bot_id == pick, -jnp.inf, bot)
        win = ids[0] == pick
        for b in range(n - 1 - t):
            vals[b] = jnp.where(win, vals[b + 1], vals[b])
            ids[b] = jnp.where(win, ids[b + 1], ids[b])
    return out_v, out_i


def _route_kernel(st_ref, i1_out, i2_out, g_out):
    w = st_ref.shape[1]
    k = PEER_TOPK

    def head(h, _):
        base = pl.multiple_of(h * 2 * N_KEYS, 2 * N_KEYS)
        v1, i1, _ = _top_keys(st_ref, base, k)
        _, i2, v2rows = _top_keys(st_ref, base + N_KEYS, k)
        best, best_id = _top_pair_sums(v1, v2rows, k)
        a_sel = best_id >> _TOPK_SHIFT
        b_sel = best_id & (k - 1)
        e1 = jnp.zeros((k, w), jnp.int32)
        e2 = jnp.zeros((k, w), jnp.int32)
        for c in range(k):
            e1 = jnp.where(a_sel == c, i1[c:c + 1, :], e1)
            e2 = jnp.where(b_sel == c, i2[c:c + 1, :], e2)
        ex = jnp.exp(best - best[0:1, :])
        gate = ex / jnp.sum(ex, axis=0, keepdims=True)
        o = pl.multiple_of(h * k, k)
        i1_out[pl.ds(o, k), :] = e1
        i2_out[pl.ds(o, k), :] = e2
        g_out[pl.ds(o, k), :] = gate
        return 0

    lax.fori_loop(0, PEER_HEADS, head, 0, unroll=4)


def _route(st, *, tl):
    b, nsc, s = st.shape
    nj = PEER_HEADS * PEER_TOPK
    grid = (b, s // tl)
    ospec = pl.BlockSpec((None, nj, tl), lambda bi, i: (bi, 0, i))
    return pl.pallas_call(
        _route_kernel,
        grid=grid,
        in_specs=[pl.BlockSpec((None, nsc, tl), lambda bi, i: (bi, 0, i))],
        out_specs=(ospec, ospec, ospec),
        out_shape=(jax.ShapeDtypeStruct((b, nj, s), jnp.int32), jax.ShapeDtypeStruct((b, nj, s), jnp.int32),
                   jax.ShapeDtypeStruct((b, nj, s), F32)),
        compiler_params=pltpu.CompilerParams(dimension_semantics=("arbitrary", "arbitrary")),
        name="route",
    )(st)


_GROUP = 16
_PITCH = N_KEYS + 4


def _scatter_kernel(i1_ref, i2_ref, g_ref, gd_out, stage_a, stage_b):
    tg = i1_ref.shape[0]
    nj = i1_ref.shape[1]
    sub = lax.broadcasted_iota(jnp.int32, (N_KEYS, nj), 0)
    stages = (stage_a, stage_b)

    def fill(grp):
        stage = stages[grp % 2]
        for t in range(_GROUP):
            tok = grp * _GROUP + t
            r1 = i1_ref[tok:tok + 1, :]
            r2 = i2_ref[tok:tok + 1, :]
            gg = g_ref[tok:tok + 1, :]
            p1 = jnp.where(sub == r1, gg, 0.0).astype(BF16)
            p2 = jnp.where(sub == r2, 1.0, 0.0).astype(BF16)
            stage[t * _PITCH:t * _PITCH + N_KEYS, :] = _dot_nt(p1, p2)

    def drain(grp):
        stage = stages[grp % 2]
        for a in range(N_KEYS):
            rows = stage[pl.ds(a, _GROUP, stride=_PITCH), :]
            gd_out[grp * _GROUP:(grp + 1) * _GROUP, a * N_KEYS:(a + 1) * N_KEYS] = rows.astype(BF16)

    ngroups = tg // _GROUP
    fill(0)
    for grp in range(ngroups):
        if grp + 1 < ngroups:
            fill(grp + 1)
        drain(grp)


def _scatter(i1, i2, g, *, tg):
    t, nj = i1.shape
    ne = N_KEYS * N_KEYS
    spec = pl.BlockSpec((tg, nj), lambda i: (i, 0))
    return pl.pallas_call(
        _scatter_kernel,
        grid=(t // tg,),
        in_specs=[spec, spec, spec],
        out_specs=pl.BlockSpec((tg, ne), lambda i: (i, 0)),
        out_shape=jax.ShapeDtypeStruct((t, ne), BF16),
        scratch_shapes=[pltpu.VMEM((_GROUP * _PITCH, N_KEYS), F32), pltpu.VMEM((_GROUP * _PITCH, N_KEYS), F32)],
        compiler_params=pltpu.CompilerParams(dimension_semantics=("arbitrary",)),
        name="scatter",
    )(i1, i2, g)


def _peer_kernel(unscale_ref, xq_ref, h1_ref, u_ref, v_ref, gd_ref, ln2g_ref, ln2b_ref, o_ref):
    e = pl.program_id(1)

    @pl.when(e == 0)
    def _():
        o_ref[...] = jnp.zeros_like(o_ref)

    raw = _dot_nt(xq_ref[...], u_ref[...])
    w = 1.0 + lax.erf(raw * (unscale_ref[0] * 2.0 ** -0.5))
    hd = (raw * w * gd_ref[...].astype(F32)).astype(F8)
    o_ref[...] += _dot(hd, v_ref[...])

    @pl.when(e == pl.num_programs(1) - 1)
    def _():
        f = o_ref[...] * unscale_ref[1]
        o_ref[...] = _layer_norm(DEEPNORM_ALPHA * h1_ref[...] + f, ln2g_ref[...], ln2b_ref[...])


def _peer(unscale, xq, h1, u, v, gd, ln2g, ln2b, *, tm, ec):
    t, d = xq.shape
    ne = v.shape[0]
    grid = (t // tm, ne // ec)
    return pl.pallas_call(
        _peer_kernel,
        grid=grid,
        in_specs=[
            pl.BlockSpec(memory_space=pltpu.SMEM),
            pl.BlockSpec((tm, d), lambda i, e: (i, 0)),
            pl.BlockSpec((tm, d), lambda i, e: (i, 0), pipeline_mode=pl.Buffered(1)),
            pl.BlockSpec((ec, d), lambda i, e: (e, 0)),
            pl.BlockSpec((ec, d), lambda i, e: (e, 0)),
            pl.BlockSpec((tm, ec), lambda i, e: (i, e)),
            _const_spec(ln2g.shape), _const_spec(ln2b.shape),
        ],
        out_specs=pl.BlockSpec((tm, d), lambda i, e: (i, 0)),
        out_shape=jax.ShapeDtypeStruct((t, d), F32),
        compiler_params=pltpu.CompilerParams(
            dimension_semantics=("arbitrary", "arbitrary"), vmem_limit_bytes=VMEM_LIMIT),
        name="peer",
    )(unscale, xq, h1, u, v, gd, ln2g, ln2b)


def _pow2_below(x):
    return jnp.exp2(jnp.floor(jnp.log2(x)))


def _peer_scales(ln1_g, ln1_b, peer_u, peer_v, d):
    tiny = jnp.float32(1e-30)
    h_elem = jnp.max(jnp.sqrt(float(d)) * jnp.abs(ln1_g) + jnp.abs(ln1_b))
    h_norm = jnp.sqrt(float(d)) * jnp.max(jnp.abs(ln1_g)) + jnp.sqrt(jnp.sum(ln1_b * ln1_b))
    u_row = jnp.sqrt(jnp.max(jnp.sum(peer_u * peer_u, axis=1)))
    u_max = jnp.max(jnp.max(jnp.abs(peer_u), axis=1))
    sx = _pow2_below(_F8_TARGET / jnp.maximum(h_elem, tiny))
    su = _pow2_below(_F8_TARGET / jnp.maximum(u_max, tiny))
    sv = _pow2_below(_F8_TARGET / jnp.maximum(jnp.max(jnp.abs(peer_v)), tiny))
    sh = _pow2_below(_F8_TARGET / jnp.maximum(h_norm * u_row, tiny))
    return sx, su, sv, sh


def _rope_tables(first_pos, n):
    inv = 1.0 / (ROPE_THETA ** (jnp.arange(0, D_ROPE, 2, dtype=F32) / D_ROPE))
    ang = (first_pos + jnp.arange(n, dtype=F32))[:, None] * inv[None, :]
    cos, sin = jnp.cos(ang), jnp.sin(ang)
    z = jnp.zeros((n, LANES - D_ROPE), F32)
    return jnp.concatenate([cos, cos, z], axis=1), jnp.concatenate([-sin, sin, z], axis=1)


def _swap_halves(w):
    half = w.shape[-1] // 2
    return jnp.concatenate([w[..., half:], w[..., :half]], axis=-1)


def _pick(n, prefs):
    for p in prefs:
        if n % p == 0:
            return p
    raise ValueError(f"no tile in {prefs} divides {n}")


def kernel(x, meta_tokens, ln0_g, ln0_b, w_in, q_norm_g, kv_norm_g, w_uq, w_ukv, conv_w, attn_out_g, conv_out_g,
           w_o, ln1_g, ln1_b, peer_w_query, peer_sub_keys, peer_u, peer_v, ln2_g, ln2_b):
    b, s, d = x.shape
    assert w_in.shape[0] == 1 and s % _Q_TILE == 0
    conv_dim = conv_w.shape[2]
    h = MLA_HEADS
    row = lambda a: a.reshape(1, -1).astype(F32)

    wi = w_in[0]
    o_kv, o_kr, o_b = Q_RANK, Q_RANK + KV_RANK, Q_RANK + KV_RANK + D_ROPE
    w_kr = wi[:, o_kr:o_b]
    zpad = jnp.zeros((d, LANES - D_ROPE), F32)
    win = jnp.concatenate(
        [wi[:, :o_kr], w_kr, zpad, _swap_halves(w_kr), zpad, wi[:, o_b:]], axis=1).astype(BF16)
    wq3 = w_uq[0].reshape(Q_RANK, h, D_QK)
    zq = jnp.zeros((Q_RANK, h, QK_PAD - D_QK), F32)
    wqat = jnp.concatenate([wq3, zq], axis=2).reshape(Q_RANK, h * QK_PAD).T.astype(BF16)
    wqbt = jnp.concatenate([_swap_halves(wq3[:, :, D_NOPE:]), zq], axis=2).reshape(Q_RANK, h * LANES).T.astype(BF16)
    wkv3 = w_ukv[0].reshape(KV_RANK, h, D_NOPE + D_V)
    wk = wkv3[:, :, :D_NOPE].reshape(KV_RANK, h * D_NOPE).astype(BF16)
    wvt = wkv3[:, :, D_NOPE:].reshape(KV_RANK, h * D_V).T.astype(BF16)
    wo = w_o[0].astype(BF16)
    wpq = peer_w_query[0].astype(BF16)
    sk = peer_sub_keys[0].astype(BF16)
    sx, su, sv, sh = _peer_scales(ln1_g[0], ln1_b[0], peer_u[0], peer_v[0], d)
    pu = (peer_u[0] * su).astype(F8)
    pv = (peer_v[0] * sv).astype(F8)
    unscale = jnp.stack([1.0 / (sx * su), 1.0 / (sh * sv)]).astype(F32)

    inproj_w = (row(ln0_g), row(ln0_b), win, row(q_norm_g[0]), row(kv_norm_g[0]), wqat, wqbt, wk, wvt,
                conv_w[0].astype(F32), row(conv_out_g[0]))

    t1m, t2m = _rope_tables(0, N_META)
    _, km, vmt, _, utail = _inproj(meta_tokens[None].astype(F32), t1m, t2m, t1m.T, t2m.T,
                                   jnp.zeros((8, conv_dim), F32), *inproj_w, tm=N_META)
    km = jnp.pad(km[0], ((0, 0), (0, LANES - N_META), (0, 0)))
    vmt = jnp.pad(vmt[0, :, 0], ((0, 0), (0, 0), (0, LANES - N_META)))

    t1, t2 = _rope_tables(N_META, s)
    qt, k, vt, convn, _ = _inproj(x, t1, t2, t1.T, t2.T, utail[0], *inproj_w, tm=_KV_BLOCK)
    oa = _attention(qt, k, vt, km, vmt, tq=_Q_TILE)
    h1, h1q, st = _outproj(sx.reshape(1), x, oa, convn, row(ln0_g), row(ln0_b), row(attn_out_g[0]), wo,
                           row(ln1_g[0]), row(ln1_b[0]), wpq, sk, tm=_pick(s, (256, 128)))
    i1, i2, g = _route(st, tl=LANES)
    tok = lambda a: jnp.swapaxes(a, 1, 2).reshape(b * s, a.shape[1])
    gd = _scatter(tok(i1), tok(i2), tok(g) * (sh * 0.5 * unscale[0]), tg=_pick(b * s, (256, 128, 64)))
    out = _peer(unscale, h1q.reshape(b * s, d), h1.reshape(b * s, d), pu, pv, gd, row(ln2_g[0]), row(ln2_b[0]),
                tm=_pick(b * s, (1024, 512)), ec=1024)
    return out.reshape(b, s, d)
```

```python
import functools

import jax
import jax.numpy as jnp
from jax import lax
from jax.experimental import pallas as pl
from jax.experimental.pallas import tpu as pltpu

CHUNK = 64
N_META = 16
MLA_HEADS = 8
D_NOPE = 128
D_ROPE = 64
D_QK = D_NOPE + D_ROPE
D_V = 128
Q_RANK = 384
KV_RANK = 512
ROPE_THETA = 10000.0
ATTN_DIM = MLA_HEADS * D_V
CONV_WIDTH = 3
PEER_HEADS = 8
N_KEYS = 128
D_HALF = 128
PEER_TOPK = 16
DEPTH = 1
DEEPNORM_ALPHA = (2.0 * DEPTH) ** 0.25
EPS = 1e-5
NEG_INF = -1e30

_CHUNK_SHIFT = CHUNK.bit_length() - 1
_TOPK_SHIFT = PEER_TOPK.bit_length() - 1
assert 1 << _CHUNK_SHIFT == CHUNK and 1 << _TOPK_SHIFT == PEER_TOPK

LANES = 128
QK_PAD = 256
_KV_BLOCK = 512
_Q_BLOCK = 256
_Q_TILE = 2 * _KV_BLOCK
_PAIRS_PER_TRIP = 2
_DENOM_ROWS = 16
_PEER_CHUNK = 1024
_LOG2E = 1.4426950408889634
VMEM_LIMIT = 58 * 1024 * 1024

F32 = jnp.float32
BF16 = jnp.bfloat16
F8 = jnp.float8_e4m3fn
_F8_TARGET = 224.0


def _dot(a, b):
    return jnp.dot(a, b, preferred_element_type=F32)


def _dot_nt(a, b):
    return lax.dot_general(a, b, (((1,), (1,)), ((), ())), preferred_element_type=F32)


def _layer_norm(x, g, b):
    mu = jnp.mean(x, axis=-1, keepdims=True)
    xc = x - mu
    var = jnp.mean(xc * xc, axis=-1, keepdims=True)
    return xc * lax.rsqrt(var + EPS) * g + b


def _rms_norm(x, g):
    return x * lax.rsqrt(jnp.mean(x * x, axis=-1, keepdims=True) + EPS) * g


def _const_spec(shape):
    nd = len(shape)
    return pl.BlockSpec(shape, lambda *_: (0,) * nd, pipeline_mode=pl.Buffered(1))


_OFF_CQ = 0
_OFF_CKV = _OFF_CQ + Q_RANK
_OFF_KRA = _OFF_CKV + KV_RANK
_OFF_KRB = _OFF_KRA + LANES
_OFF_BG = _OFF_KRB + LANES


def _inproj_kernel(x_ref, t1_ref, t2_ref, t1t_ref, t2t_ref, uinit_ref, ln0g_ref, ln0b_ref, win_ref, qg_ref,
                   kvg_ref, wqat_ref, wqbt_ref, wk_ref, wvt_ref, convw_ref, convg_ref,
                   qt_out, k_out, vt_out, convn_out, utail_out, ubuf, *, tm, vb, conv_dim):
    i = pl.program_id(1)
    off_cg = _OFF_BG + conv_dim
    off_hc = off_cg + conv_dim

    @pl.when(i == 0)
    def _():
        ubuf[0:8, :] = uinit_ref[...]

    hb = _layer_norm(x_ref[...], ln0g_ref[...], ln0b_ref[...]).astype(BF16)
    t1 = t1_ref[...]
    t2 = t2_ref[...]
    t1t = t1t_ref[...]
    t2t = t2t_ref[...]
    qscale = D_QK ** -0.5 * _LOG2E

    cq = _dot(hb, win_ref[:, _OFF_CQ:_OFF_CQ + Q_RANK])
    ckv = _dot(hb, win_ref[:, _OFF_CKV:_OFF_CKV + KV_RANK])
    kra = _dot(hb, win_ref[:, _OFF_KRA:_OFF_KRA + LANES])
    krb = _dot(hb, win_ref[:, _OFF_KRB:_OFF_KRB + LANES])
    krot = (kra * t1 + krb * t2).astype(BF16)
    cqn = _rms_norm(cq, qg_ref[...]).astype(BF16)
    ckvn = _rms_norm(ckv, kvg_ref[...]).astype(BF16)

    qat_all = _dot_nt(wqat_ref[...], cqn)
    qbt_all = _dot_nt(wqbt_ref[...], cqn)
    vt_all = _dot_nt(wvt_ref[...], ckvn).astype(BF16)
    kn_all = _dot(ckvn, wk_ref[...]).astype(BF16)
    for h in range(MLA_HEADS):
        qat = qat_all[h * QK_PAD:(h + 1) * QK_PAD, :]
        qbt = qbt_all[h * LANES:(h + 1) * LANES, :]
        qt_out[h, 0:LANES, :] = (qat[0:LANES, :] * qscale).astype(BF16)
        qt_out[h, LANES:QK_PAD, :] = ((qat[LANES:QK_PAD, :] * t1t + qbt * t2t) * qscale).astype(BF16)
        k_out[h, :, 0:LANES] = kn_all[:, h * LANES:(h + 1) * LANES]
        k_out[h, :, LANES:QK_PAD] = krot
        vt = vt_all[h * D_V:(h + 1) * D_V, :]
        for bk in range(tm // vb):
            vt_out[h, bk] = vt[:, bk * vb:(bk + 1) * vb]

    bg = _dot(hb, win_ref[:, _OFF_BG:_OFF_BG + conv_dim])
    cg = _dot(hb, win_ref[:, off_cg:off_cg + conv_dim])
    hc = _dot(hb, win_ref[:, off_hc:off_hc + conv_dim])
    u = cg * hc
    ubuf[8:8 + tm, :] = u
    u1 = ubuf[7:7 + tm, :]
    u2 = ubuf[6:6 + tm, :]
    w = convw_ref[...]
    y = bg * (u2 * w[0:1, :] + u1 * w[1:2, :] + u * w[2:3, :])
    convn_out[...] = _rms_norm(y, convg_ref[...]).astype(BF16)
    tail = ubuf[tm:tm + 8, :]
    utail_out[...] = tail
    ubuf[0:8, :] = tail


def _inproj(x, t1, t2, t1t, t2t, uinit, ln0g, ln0b, win, qg, kvg, wqat, wqbt, wk, wvt, convw, convg, *, tm):
    b, s, d = x.shape
    conv_dim = convw.shape[1]
    grid = (b, s // tm)
    h = MLA_HEADS
    vb = min(tm, _KV_BLOCK)
    out_shape = (
        jax.ShapeDtypeStruct((b, h, QK_PAD, s), BF16),
        jax.ShapeDtypeStruct((b, h, s, QK_PAD), BF16),
        jax.ShapeDtypeStruct((b, h, s // vb, D_V, vb), BF16),
        jax.ShapeDtypeStruct((b, s, conv_dim), BF16),
        jax.ShapeDtypeStruct((b, 8, conv_dim), F32),
    )
    in_specs = [
        pl.BlockSpec((None, tm, d), lambda bi, i: (bi, i, 0)),
        pl.BlockSpec((tm, LANES), lambda bi, i: (i, 0)),
        pl.BlockSpec((tm, LANES), lambda bi, i: (i, 0)),
        pl.BlockSpec((LANES, tm), lambda bi, i: (0, i)),
        pl.BlockSpec((LANES, tm), lambda bi, i: (0, i)),
        _const_spec(uinit.shape), _const_spec(ln0g.shape), _const_spec(ln0b.shape), _const_spec(win.shape),
        _const_spec(qg.shape), _const_spec(kvg.shape), _const_spec(wqat.shape), _const_spec(wqbt.shape),
        _const_spec(wk.shape), _const_spec(wvt.shape), _const_spec(convw.shape), _const_spec(convg.shape),
    ]
    out_specs = (
        pl.BlockSpec((None, h, QK_PAD, tm), lambda bi, i: (bi, 0, 0, i)),
        pl.BlockSpec((None, h, tm, QK_PAD), lambda bi, i: (bi, 0, i, 0)),
        pl.BlockSpec((None, h, tm // vb, D_V, vb), lambda bi, i: (bi, 0, i, 0, 0)),
        pl.BlockSpec((None, tm, conv_dim), lambda bi, i: (bi, i, 0)),
        pl.BlockSpec((None, 8, conv_dim), lambda bi, i: (bi, 0, 0)),
    )
    return pl.pallas_call(
        functools.partial(_inproj_kernel, tm=tm, vb=vb, conv_dim=conv_dim),
        grid=grid, in_specs=in_specs, out_specs=out_specs, out_shape=out_shape,
        scratch_shapes=[pltpu.VMEM((tm + 8, conv_dim), F32)],
        compiler_params=pltpu.CompilerParams(
            dimension_semantics=("arbitrary", "arbitrary"), vmem_limit_bytes=VMEM_LIMIT),
        name="inproj",
    )(x, t1, t2, t1t, t2t, uinit, ln0g, ln0b, win, qg, kvg, wqat, wqbt, wk, wvt, convw, convg)


def _attn_kernel(qt_ref, k_ref, vt_ref, km_ref, vmt_ref, o_ref, acc_ref, sa_ref, sb_ref, *, tq):
    i = pl.program_id(2)
    kb, qb = _KV_BLOCK, _Q_BLOCK
    ncb = tq // qb
    nd = tq // kb
    assert nd == 2, "the two-buffer pipeline consumes key blocks in pairs"

    def cols(cb):
        return slice(cb * qb, (cb + 1) * qb)

    def with_ones(vt):
        return jnp.concatenate([vt, jnp.ones((_DENOM_ROWS, vt.shape[1]), BF16)], axis=0)

    def absorb(cb, m, s, smax, vt1):
        mn = jnp.maximum(m, smax)
        a = jnp.exp2(m - mn)
        p = jnp.exp2(s - mn)
        acc_ref[:, cols(cb)] = a * acc_ref[:, cols(cb)] + _dot(vt1, p.astype(BF16))
        return mn

    def stash(s_ref, k, cb):
        s = _dot(k, qt_ref[:, cols(cb)])
        s_ref[:, cols(cb)] = s
        return jnp.max(s, axis=0, keepdims=True)

    def key_block(j):
        return k_ref[pl.ds(pl.multiple_of(j * kb, kb), kb), :]

    def trade(carry, s_old, smax_old, vt_old, s_new, j_new):
        k_new = key_block(j_new)
        vt1 = with_ones(vt_old)
        ms, smax_new = list(carry), []
        for cb in range(ncb):
            s = s_old[:, cols(cb)]
            smax_new.append(stash(s_new, k_new, cb))
            ms[cb] = absorb(cb, ms[cb], s, smax_old[cb], vt1)
        return tuple(ms), tuple(smax_new)

    k0 = key_block(0)
    max_a = tuple(stash(sa_ref, k0, cb) for cb in range(ncb))

    ms = []
    meta_rows = lax.broadcasted_iota(jnp.int32, (km_ref.shape[0], qb), 0) < N_META
    vmt1 = with_ones(vmt_ref[...])
    for cb in range(ncb):
        s = jnp.where(meta_rows, _dot(km_ref[...], qt_ref[:, cols(cb)]), NEG_INF)
        m0 = jnp.max(s, axis=0, keepdims=True)
        ms.append(m0)
        acc_ref[:, cols(cb)] = _dot(vmt1, jnp.exp2(s - m0).astype(BF16))

    def pair(j, carry):
        stats, max_a = carry
        stats, max_b = trade(stats, sa_ref, max_a, vt_ref[j], sb_ref, j + 1)
        return trade(stats, sb_ref, max_b, vt_ref[j + 1], sa_ref, j + 2)

    def pairs(t, c):
        for u in range(_PAIRS_PER_TRIP):
            c = pair(2 * (_PAIRS_PER_TRIP * t + u), c)
        return c

    whole = i // _PAIRS_PER_TRIP
    carry = lax.fori_loop(0, whole, pairs, (tuple(ms), max_a))
    ms, _ = lax.fori_loop(whole * _PAIRS_PER_TRIP, i, lambda jp, c: pair(2 * jp, c), carry)
    ms = list(ms)

    def masked(s, jj, cb):
        if (jj + 1) * kb <= cb * qb + CHUNK:
            return s
        key_chunk = (lax.broadcasted_iota(jnp.int32, (kb, qb), 0) + jj * kb) >> _CHUNK_SHIFT
        qry_chunk = (lax.broadcasted_iota(jnp.int32, (kb, qb), 1) + cb * qb) >> _CHUNK_SHIFT
        return jnp.where(key_chunk <= qry_chunk, s, NEG_INF)

    j0 = i * nd
    k_last = key_block(j0 + 1)
    second_from = kb // qb
    vt1 = with_ones(vt_ref[j0])
    for cb in range(ncb):
        s = masked(sa_ref[:, cols(cb)], 0, cb)
        if cb >= second_from:
            sb_ref[:, cols(cb)] = _dot(k_last, qt_ref[:, cols(cb)])
        ms[cb] = absorb(cb, ms[cb], s, jnp.max(s, axis=0, keepdims=True), vt1)
    vt1 = with_ones(vt_ref[j0 + 1])
    for cb in range(second_from, ncb):
        s = masked(sb_ref[:, cols(cb)], 1, cb)
        ms[cb] = absorb(cb, ms[cb], s, jnp.max(s, axis=0, keepdims=True), vt1)

    for cb in range(ncb):
        o = acc_ref[0:D_V, cols(cb)] / acc_ref[D_V:D_V + 1, cols(cb)]
        o_ref[cb * qb:(cb + 1) * qb, :] = o.T.astype(BF16)


def _attention(qt, k, vt, km, vmt, *, tq):
    b, h, _, s = qt.shape
    grid = (b, h, s // tq)
    nkb = vt.shape[2]
    return pl.pallas_call(
        functools.partial(_attn_kernel, tq=tq),
        grid=grid,
        in_specs=[
            pl.BlockSpec((None, None, QK_PAD, tq), lambda bi, hi, i: (bi, hi, 0, i)),
            pl.BlockSpec((None, None, s, QK_PAD), lambda bi, hi, i: (bi, hi, 0, 0)),
            pl.BlockSpec((None, None, nkb, D_V, _KV_BLOCK), lambda bi, hi, i: (bi, hi, 0, 0, 0)),
            pl.BlockSpec((None, LANES, QK_PAD), lambda bi, hi, i: (hi, 0, 0)),
            pl.BlockSpec((None, D_V, LANES), lambda bi, hi, i: (hi, 0, 0)),
        ],
        out_specs=pl.BlockSpec((None, tq, D_V), lambda bi, hi, i: (bi, i, hi)),
        out_shape=jax.ShapeDtypeStruct((b, s, h * D_V), BF16),
        scratch_shapes=[pltpu.VMEM((D_V + _DENOM_ROWS, tq), F32), pltpu.VMEM((_KV_BLOCK, tq), F32),
                        pltpu.VMEM((_KV_BLOCK, tq), F32)],
        compiler_params=pltpu.CompilerParams(
            dimension_semantics=("arbitrary", "arbitrary", "arbitrary"), vmem_limit_bytes=VMEM_LIMIT),
        name="attn",
    )(qt, k, vt, km, vmt)


def _outproj_kernel(xscale_ref, x_ref, oa_ref, cn_ref, ln0g_ref, ln0b_ref, ag_ref, wo_ref, ln1g_ref, ln1b_ref,
                    wq_ref, sk_ref, h1_out, h1q_out, st_out):
    h0 = _layer_norm(x_ref[...], ln0g_ref[...], ln0b_ref[...])
    an = _rms_norm(oa_ref[...].astype(F32), ag_ref[...]).astype(BF16)
    mixed = _dot(an, wo_ref[0:ATTN_DIM, :]) + _dot(cn_ref[...], wo_ref[ATTN_DIM:, :])
    h1 = _layer_norm(DEEPNORM_ALPHA * h0 + mixed, ln1g_ref[...], ln1b_ref[...])
    h1_out[...] = h1
    h1q_out[...] = (h1 * xscale_ref[0]).astype(F8)
    h1b = h1.astype(BF16)
    pq = _dot(h1b, wq_ref[...])
    for hc in range(PEER_HEADS * 2):
        pqs = pq[:, hc * D_HALF:(hc + 1) * D_HALF].astype(BF16)
        st_out[hc * N_KEYS:(hc + 1) * N_KEYS, :] = _dot_nt(sk_ref[hc % 2], pqs)


def _outproj(xscale, x, oa, cn, ln0g, ln0b, ag, wo, ln1g, ln1b, wq, sk, *, tm):
    b, s, d = x.shape
    grid = (b, s // tm)
    nsc = PEER_HEADS * 2 * N_KEYS
    tok = lambda w: pl.BlockSpec((None, tm, w), lambda bi, i: (bi, i, 0))
    return pl.pallas_call(
        _outproj_kernel,
        grid=grid,
        in_specs=[pl.BlockSpec(memory_space=pltpu.SMEM), tok(d), tok(oa.shape[2]), tok(cn.shape[2]),
                  _const_spec(ln0g.shape), _const_spec(ln0b.shape), _const_spec(ag.shape), _const_spec(wo.shape),
                  _const_spec(ln1g.shape), _const_spec(ln1b.shape), _const_spec(wq.shape), _const_spec(sk.shape)],
        out_specs=(tok(d), tok(d), pl.BlockSpec((None, nsc, tm), lambda bi, i: (bi, 0, i))),
        out_shape=(jax.ShapeDtypeStruct((b, s, d), F32), jax.ShapeDtypeStruct((b, s, d), F8),
                   jax.ShapeDtypeStruct((b, nsc, s), F32)),
        compiler_params=pltpu.CompilerParams(
            dimension_semantics=("arbitrary", "arbitrary"), vmem_limit_bytes=VMEM_LIMIT),
        name="outproj",
    )(xscale, x, oa, cn, ln0g, ln0b, ag, wo, ln1g, ln1b, wq, sk)


_BIG_ID = 1 << 20


_SUBLANES = 8


def _top_keys(s_ref, base, n):
    w = s_ref.shape[1]
    nv = N_KEYS // _SUBLANES
    assert n <= nv
    sub = lax.broadcasted_iota(jnp.int32, (_SUBLANES, w), 0)
    vals = [s_ref[pl.ds(pl.multiple_of(base + r * _SUBLANES, _SUBLANES), _SUBLANES), :] for r in range(nv)]
    ids = [sub + r * _SUBLANES for r in range(nv)]
    for rnd in range(nv):
        for i in range(rnd % 2, nv - 1, 2):
            swap = vals[i + 1] > vals[i]
            hi, lo = jnp.maximum(vals[i], vals[i + 1]), jnp.minimum(vals[i], vals[i + 1])
            ids[i], ids[i + 1] = jnp.where(swap, ids[i + 1], ids[i]), jnp.where(swap, ids[i], ids[i + 1])
            vals[i], vals[i + 1] = hi, lo
    rows = lax.broadcasted_iota(jnp.int32, (n, w), 0)
    out_v = jnp.zeros((n, w), F32)
    out_i = jnp.zeros((n, w), jnp.int32)
    val_rows = []
    for t in range(n):
        m = jnp.max(vals[0], axis=0, keepdims=True)
        pick = jnp.min(jnp.where(vals[0] == m, ids[0], _BIG_ID), axis=0, keepdims=True)
        out_v = jnp.where(rows == t, m, out_v)
        out_i = jnp.where(rows == t, pick, out_i)
        val_rows.append(m)
        win = ids[0] == pick
        for r in range(n - 1 - t):
            vals[r] = jnp.where(win, vals[r + 1], vals[r])
            ids[r] = jnp.where(win, ids[r + 1], ids[r])
    return out_v, out_i, val_rows


def _top_pair_sums(v1, v2rows, n):
    w = v1.shape[1]
    half = n // 2
    assert half == _SUBLANES
    ra = lax.broadcasted_iota(jnp.int32, (half, w), 0)
    top = v1[0:half, :]
    bot = v1[half:, :] + v2rows[0]
    bot_id = (ra + half) * n
    vals = [jnp.where(ra < n // (b + 1), top + v2rows[b], -jnp.inf) for b in range(n)]
    ids = [ra * n + b for b in range(n)]
    rows = lax.broadcasted_iota(jnp.int32, (n, w), 0)
    out_v = jnp.zeros((n, w), F32)
    out_i = jnp.zeros((n, w), jnp.int32)
    for t in range(n):
        m = jnp.max(jnp.maximum(vals[0], bot), axis=0, keepdims=True)
        lowest = jnp.minimum(jnp.where(vals[0] == m, ids[0], _BIG_ID), jnp.where(bot == m, bot_id, _BIG_ID))
        pick = jnp.min(lowest, axis=0, keepdims=True)
        out_v = jnp.where(rows == t, m, out_v)
        out_i = jnp.where(rows == t, pick, out_i)
        bot = jnp.where(bot_id == pick, -jnp.inf, bot)
        win = ids[0] == pick
        for b in range(n - 1 - t):
            vals[b] = jnp.where(win, vals[b + 1], vals[b])
            ids[b] = jnp.where(win, ids[b + 1], ids[b])
    return out_v, out_i


def _route_kernel(st_ref, i1_out, i2_out, g_out):
    w = st_ref.shape[1]
    k = PEER_TOPK

    def head(h, _):
        base = pl.multiple_of(h * 2 * N_KEYS, 2 * N_KEYS)
        v1, i1, _ = _top_keys(st_ref, base, k)
        _, i2, v2rows = _top_keys(st_ref, base + N_KEYS, k)
        best, best_id = _top_pair_sums(v1, v2rows, k)
        a_sel = best_id >> _TOPK_SHIFT
        b_sel = best_id & (k - 1)
        e1 = jnp.zeros((k, w), jnp.int32)
        e2 = jnp.zeros((k, w), jnp.int32)
        for c in range(k):
            e1 = jnp.where(a_sel == c, i1[c:c + 1, :], e1)
            e2 = jnp.where(b_sel == c, i2[c:c + 1, :], e2)
        ex = jnp.exp(best - best[0:1, :])
        gate = ex / jnp.sum(ex, axis=0, keepdims=True)
        o = pl.multiple_of(h * k, k)
        i1_out[pl.ds(o, k), :] = e1
        i2_out[pl.ds(o, k), :] = e2
        g_out[pl.ds(o, k), :] = gate
        return 0

    lax.fori_loop(0, PEER_HEADS, head, 0, unroll=4)


def _route(st, *, tl):
    b, nsc, s = st.shape
    nj = PEER_HEADS * PEER_TOPK
    grid = (b, s // tl)
    ospec = pl.BlockSpec((None, nj, tl), lambda bi, i: (bi, 0, i))
    return pl.pallas_call(
        _route_kernel,
        grid=grid,
        in_specs=[pl.BlockSpec((None, nsc, tl), lambda bi, i: (bi, 0, i))],
        out_specs=(ospec, ospec, ospec),
        out_shape=(jax.ShapeDtypeStruct((b, nj, s), jnp.int32), jax.ShapeDtypeStruct((b, nj, s), jnp.int32),
                   jax.ShapeDtypeStruct((b, nj, s), F32)),
        compiler_params=pltpu.CompilerParams(dimension_semantics=("arbitrary", "arbitrary")),
        name="route",
    )(st)


_GROUP = 16
_PITCH = N_KEYS + 4


def _scatter_kernel(i1_ref, i2_ref, g_ref, gd_out, stage_a, stage_b):
    tg = i1_ref.shape[0]
    nj = i1_ref.shape[1]
    sub = lax.broadcasted_iota(jnp.int32, (N_KEYS, nj), 0)
    stages = (stage_a, stage_b)

    def fill(grp):
        stage = stages[grp % 2]
        for t in range(_GROUP):
            tok = grp * _GROUP + t
            r1 = i1_ref[tok:tok + 1, :]
            r2 = i2_ref[tok:tok + 1, :]
            gg = g_ref[tok:tok + 1, :]
            p1 = jnp.where(sub == r1, gg, 0.0).astype(BF16)
            p2 = jnp.where(sub == r2, 1.0, 0.0).astype(BF16)
            stage[t * _PITCH:t * _PITCH + N_KEYS, :] = _dot_nt(p1, p2)

    def drain(grp):
        stage = stages[grp % 2]
        for a in range(N_KEYS):
            rows = stage[pl.ds(a, _GROUP, stride=_PITCH), :]
            gd_out[grp * _GROUP:(grp + 1) * _GROUP, a * N_KEYS:(a + 1) * N_KEYS] = rows.astype(BF16)

    ngroups = tg // _GROUP
    fill(0)
    for grp in range(ngroups):
        if grp + 1 < ngroups:
            fill(grp + 1)
        drain(grp)


def _scatter(i1, i2, g, *, tg):
    t, nj = i1.shape
    ne = N_KEYS * N_KEYS
    spec = pl.BlockSpec((tg, nj), lambda i: (i, 0))
    return pl.pallas_call(
        _scatter_kernel,
        grid=(t // tg,),
        in_specs=[spec, spec, spec],
        out_specs=pl.BlockSpec((tg, ne), lambda i: (i, 0)),
        out_shape=jax.ShapeDtypeStruct((t, ne), BF16),
        scratch_shapes=[pltpu.VMEM((_GROUP * _PITCH, N_KEYS), F32), pltpu.VMEM((_GROUP * _PITCH, N_KEYS), F32)],
        compiler_params=pltpu.CompilerParams(dimension_semantics=("arbitrary",)),
        name="scatter",
    )(i1, i2, g)


def _peer_kernel(unscale_ref, xq_ref, h1_ref, u_ref, v_ref, gd_ref, ln2g_ref, ln2b_ref, o_ref):
    e = pl.program_id(1)

    @pl.when(e == 0)
    def _():
        o_ref[...] = jnp.zeros_like(o_ref)

    raw = _dot_nt(xq_ref[...], u_ref[...])
    w = 1.0 + lax.erf(raw * (unscale_ref[0] * 2.0 ** -0.5))
    hd = (raw * w * gd_ref[...].astype(F32)).astype(F8)
    o_ref[...] += _dot(hd, v_ref[...])

    @pl.when(e == pl.num_programs(1) - 1)
    def _():
        f = o_ref[...] * unscale_ref[1]
        o_ref[...] = _layer_norm(DEEPNORM_ALPHA * h1_ref[...] + f, ln2g_ref[...], ln2b_ref[...])


def _peer(unscale, xq, h1, u, v, gd, ln2g, ln2b, *, tm, ec):
    t, d = xq.shape
    ne = v.shape[0]
    grid = (t // tm, ne // ec)
    return pl.pallas_call(
        _peer_kernel,
        grid=grid,
        in_specs=[
            pl.BlockSpec(memory_space=pltpu.SMEM),
            pl.BlockSpec((tm, d), lambda i, e: (i, 0)),
            pl.BlockSpec((tm, d), lambda i, e: (i, 0), pipeline_mode=pl.Buffered(1)),
            pl.BlockSpec((ec, d), lambda i, e: (e, 0)),
            pl.BlockSpec((ec, d), lambda i, e: (e, 0)),
            pl.BlockSpec((tm, ec), lambda i, e: (i, e)),
            _const_spec(ln2g.shape), _const_spec(ln2b.shape),
        ],
        out_specs=pl.BlockSpec((tm, d), lambda i, e: (i, 0)),
        out_shape=jax.ShapeDtypeStruct((t, d), F32),
        compiler_params=pltpu.CompilerParams(
            dimension_semantics=("arbitrary", "arbitrary"), vmem_limit_bytes=VMEM_LIMIT),
        name="peer",
    )(unscale, xq, h1, u, v, gd, ln2g, ln2b)


def _pow2_below(x):
    return jnp.exp2(jnp.floor(jnp.log2(x)))


def _peer_scales(ln1_g, ln1_b, peer_u, peer_v, d):
    tiny = jnp.float32(1e-30)
    h_elem = jnp.max(jnp.sqrt(float(d)) * jnp.abs(ln1_g) + jnp.abs(ln1_b))
    h_norm = jnp.sqrt(float(d)) * jnp.max(jnp.abs(ln1_g)) + jnp.sqrt(jnp.sum(ln1_b * ln1_b))
    u_row = jnp.sqrt(jnp.max(jnp.sum(peer_u * peer_u, axis=1)))
    u_max = jnp.max(jnp.max(jnp.abs(peer_u), axis=1))
    sx = _pow2_below(_F8_TARGET / jnp.maximum(h_elem, tiny))
    su = _pow2_below(_F8_TARGET / jnp.maximum(u_max, tiny))
    sv = _pow2_below(_F8_TARGET / jnp.maximum(jnp.max(jnp.abs(peer_v)), tiny))
    sh = _pow2_below(_F8_TARGET / jnp.maximum(h_norm * u_row, tiny))
    return sx, su, sv, sh


def _rope_tables(first_pos, n):
    inv = 1.0 / (ROPE_THETA ** (jnp.arange(0, D_ROPE, 2, dtype=F32) / D_ROPE))
    ang = (first_pos + jnp.arange(n, dtype=F32))[:, None] * inv[None, :]
    cos, sin = jnp.cos(ang), jnp.sin(ang)
    z = jnp.zeros((n, LANES - D_ROPE), F32)
    return jnp.concatenate([cos, cos, z], axis=1), jnp.concatenate([-sin, sin, z], axis=1)


def _swap_halves(w):
    half = w.shape[-1] // 2
    return jnp.concatenate([w[..., half:], w[..., :half]], axis=-1)


def _pick(n, prefs):
    for p in prefs:
        if n % p == 0:
            return p
    raise ValueError(f"no tile in {prefs} divides {n}")


def kernel(x, meta_tokens, ln0_g, ln0_b, w_in, q_norm_g, kv_norm_g, w_uq, w_ukv, conv_w, attn_out_g, conv_out_g,
           w_o, ln1_g, ln1_b, peer_w_query, peer_sub_keys, peer_u, peer_v, ln2_g, ln2_b):
    b, s, d = x.shape
    assert w_in.shape[0] == DEPTH and s % _Q_TILE == 0 and conv_w.shape[1] == CONV_WIDTH
    conv_dim = conv_w.shape[2]
    h = MLA_HEADS
    row = lambda a: a.reshape(1, -1).astype(F32)

    wi = w_in[0]
    o_kr, o_b = Q_RANK + KV_RANK, Q_RANK + KV_RANK + D_ROPE
    w_kr = wi[:, o_kr:o_b]
    zpad = jnp.zeros((d, LANES - D_ROPE), F32)
    win = jnp.concatenate(
        [wi[:, :o_kr], w_kr, zpad, _swap_halves(w_kr), zpad, wi[:, o_b:]], axis=1).astype(BF16)
    wq3 = w_uq[0].reshape(Q_RANK, h, D_QK)
    zq = jnp.zeros((Q_RANK, h, QK_PAD - D_QK), F32)
    wqat = jnp.concatenate([wq3, zq], axis=2).reshape(Q_RANK, h * QK_PAD).T.astype(BF16)
    wqbt = jnp.concatenate([_swap_halves(wq3[:, :, D_NOPE:]), zq], axis=2).reshape(Q_RANK, h * LANES).T.astype(BF16)
    wkv3 = w_ukv[0].reshape(KV_RANK, h, D_NOPE + D_V)
    wk = wkv3[:, :, :D_NOPE].reshape(KV_RANK, h * D_NOPE).astype(BF16)
    wvt = wkv3[:, :, D_NOPE:].reshape(KV_RANK, h * D_V).T.astype(BF16)
    wo = w_o[0].astype(BF16)
    wpq = peer_w_query[0].astype(BF16)
    sk = peer_sub_keys[0].astype(BF16)
    sx, su, sv, sh = _peer_scales(ln1_g[0], ln1_b[0], peer_u[0], peer_v[0], d)
    pu = (peer_u[0] * su).astype(F8)
    pv = (peer_v[0] * sv).astype(F8)
    unscale = jnp.stack([1.0 / (sx * su), 1.0 / (sh * sv)]).astype(F32)

    inproj_w = (row(ln0_g), row(ln0_b), win, row(q_norm_g[0]), row(kv_norm_g[0]), wqat, wqbt, wk, wvt,
                conv_w[0].astype(F32), row(conv_out_g[0]))

    t1m, t2m = _rope_tables(0, N_META)
    _, km, vmt, _, utail = _inproj(meta_tokens[None].astype(F32), t1m, t2m, t1m.T, t2m.T,
                                   jnp.zeros((8, conv_dim), F32), *inproj_w, tm=N_META)
    km = jnp.pad(km[0], ((0, 0), (0, LANES - N_META), (0, 0)))
    vmt = jnp.pad(vmt[0, :, 0], ((0, 0), (0, 0), (0, LANES - N_META)))

    t1, t2 = _rope_tables(N_META, s)
    qt, k, vt, convn, _ = _inproj(x, t1, t2, t1.T, t2.T, utail[0], *inproj_w, tm=_KV_BLOCK)
    oa = _attention(qt, k, vt, km, vmt, tq=_Q_TILE)
    h1, h1q, st = _outproj(sx.reshape(1), x, oa, convn, row(ln0_g), row(ln0_b), row(attn_out_g[0]), wo,
                           row(ln1_g[0]), row(ln1_b[0]), wpq, sk, tm=_pick(s, (256, 128)))
    i1, i2, g = _route(st, tl=LANES)
    tok = lambda a: jnp.swapaxes(a, 1, 2).reshape(b * s, a.shape[1])
    gd = _scatter(tok(i1), tok(i2), tok(g) * (sh * 0.5 * unscale[0]), tg=_pick(b * s, (256, 128, 64)))
    out = _peer(unscale, h1q.reshape(b * s, d), h1.reshape(b * s, d), pu, pv, gd, row(ln2_g[0]), row(ln2_b[0]),
                tm=_pick(b * s, (1024, 512)), ec=_PEER_CHUNK)
    return out.reshape(b, s, d)
```

```python
import functools

import jax
import jax.numpy as jnp
from jax import lax
from jax.experimental import pallas as pl
from jax.experimental.pallas import tpu as pltpu

CHUNK = 64
N_META = 16
MLA_HEADS = 8
D_NOPE = 128
D_ROPE = 64
D_QK = D_NOPE + D_ROPE
D_V = 128
Q_RANK = 384
KV_RANK = 512
ROPE_THETA = 10000.0
ATTN_DIM = MLA_HEADS * D_V
CONV_WIDTH = 3
PEER_HEADS = 8
N_KEYS = 128
D_HALF = 128
PEER_TOPK = 16
DEPTH = 1
DEEPNORM_ALPHA = (2.0 * DEPTH) ** 0.25
EPS = 1e-5
NEG_INF = -1e30

_CHUNK_SHIFT = CHUNK.bit_length() - 1
_TOPK_SHIFT = PEER_TOPK.bit_length() - 1
assert 1 << _CHUNK_SHIFT == CHUNK and 1 << _TOPK_SHIFT == PEER_TOPK

LANES = 128
QK_PAD = 256
_KV_BLOCK = 512
_Q_BLOCK = 256
_Q_TILE = 2 * _KV_BLOCK
_PAIRS_PER_TRIP = 2
_DENOM_ROWS = 16
_PEER_CHUNK = 1024
_LOG2E = 1.4426950408889634
VMEM_LIMIT = 58 * 1024 * 1024

F32 = jnp.float32
BF16 = jnp.bfloat16
F8 = jnp.float8_e4m3fn
_F8_TARGET = 224.0


def _dot(a, b):
    return jnp.dot(a, b, preferred_element_type=F32)


def _dot_nt(a, b):
    return lax.dot_general(a, b, (((1,), (1,)), ((), ())), preferred_element_type=F32)


def _layer_norm(x, g, b):
    mu = jnp.mean(x, axis=-1, keepdims=True)
    xc = x - mu
    var = jnp.mean(xc * xc, axis=-1, keepdims=True)
    return xc * lax.rsqrt(var + EPS) * g + b


def _rms_norm(x, g):
    return x * lax.rsqrt(jnp.mean(x * x, axis=-1, keepdims=True) + EPS) * g


def _const_spec(shape):
    nd = len(shape)
    return pl.BlockSpec(shape, lambda *_: (0,) * nd, pipeline_mode=pl.Buffered(1))


_OFF_CQ = 0
_OFF_CKV = _OFF_CQ + Q_RANK
_OFF_KRA = _OFF_CKV + KV_RANK
_OFF_KRB = _OFF_KRA + LANES
_OFF_BG = _OFF_KRB + LANES


def _inproj_kernel(x_ref, t1_ref, t2_ref, t1t_ref, t2t_ref, uinit_ref, ln0g_ref, ln0b_ref, win_ref, qg_ref,
                   kvg_ref, wqat_ref, wqbt_ref, wk_ref, wvt_ref, convw_ref, convg_ref,
                   qt_out, k_out, vt_out, convn_out, utail_out, ubuf, *, tm, vb, conv_dim):
    i = pl.program_id(1)
    off_cg = _OFF_BG + conv_dim
    off_hc = off_cg + conv_dim

    @pl.when(i == 0)
    def _():
        ubuf[0:8, :] = uinit_ref[...]

    hb = _layer_norm(x_ref[...], ln0g_ref[...], ln0b_ref[...]).astype(BF16)
    t1 = t1_ref[...]
    t2 = t2_ref[...]
    t1t = t1t_ref[...]
    t2t = t2t_ref[...]
    qscale = D_QK ** -0.5 * _LOG2E

    cq = _dot(hb, win_ref[:, _OFF_CQ:_OFF_CQ + Q_RANK])
    ckv = _dot(hb, win_ref[:, _OFF_CKV:_OFF_CKV + KV_RANK])
    kra = _dot(hb, win_ref[:, _OFF_KRA:_OFF_KRA + LANES])
    krb = _dot(hb, win_ref[:, _OFF_KRB:_OFF_KRB + LANES])
    krot = (kra * t1 + krb * t2).astype(BF16)
    cqn = _rms_norm(cq, qg_ref[...]).astype(BF16)
    ckvn = _rms_norm(ckv, kvg_ref[...]).astype(BF16)

    qat_all = _dot_nt(wqat_ref[...], cqn)
    qbt_all = _dot_nt(wqbt_ref[...], cqn)
    vt_all = _dot_nt(wvt_ref[...], ckvn).astype(BF16)
    kn_all = _dot(ckvn, wk_ref[...]).astype(BF16)
    for h in range(MLA_HEADS):
        qat = qat_all[h * QK_PAD:(h + 1) * QK_PAD, :]
        qbt = qbt_all[h * LANES:(h + 1) * LANES, :]
        qt_out[h, 0:LANES, :] = (qat[0:LANES, :] * qscale).astype(BF16)
        qt_out[h, LANES:QK_PAD, :] = ((qat[LANES:QK_PAD, :] * t1t + qbt * t2t) * qscale).astype(BF16)
        k_out[h, :, 0:LANES] = kn_all[:, h * LANES:(h + 1) * LANES]
        k_out[h, :, LANES:QK_PAD] = krot
        vt = vt_all[h * D_V:(h + 1) * D_V, :]
        for bk in range(tm // vb):
            vt_out[h, bk] = vt[:, bk * vb:(bk + 1) * vb]

    bg = _dot(hb, win_ref[:, _OFF_BG:_OFF_BG + conv_dim])
    cg = _dot(hb, win_ref[:, off_cg:off_cg + conv_dim])
    hc = _dot(hb, win_ref[:, off_hc:off_hc + conv_dim])
    u = cg * hc
    ubuf[8:8 + tm, :] = u
    u1 = ubuf[7:7 + tm, :]
    u2 = ubuf[6:6 + tm, :]
    w = convw_ref[...]
    y = bg * (u2 * w[0:1, :] + u1 * w[1:2, :] + u * w[2:3, :])
    convn_out[...] = _rms_norm(y, convg_ref[...]).astype(BF16)
    tail = ubuf[tm:tm + 8, :]
    utail_out[...] = tail
    ubuf[0:8, :] = tail


def _inproj(x, t1, t2, t1t, t2t, uinit, ln0g, ln0b, win, qg, kvg, wqat, wqbt, wk, wvt, convw, convg, *, tm):
    b, s, d = x.shape
    conv_dim = convw.shape[1]
    grid = (b, s // tm)
    h = MLA_HEADS
    vb = min(tm, _KV_BLOCK)
    out_shape = (
        jax.ShapeDtypeStruct((b, h, QK_PAD, s), BF16),
        jax.ShapeDtypeStruct((b, h, s, QK_PAD), BF16),
        jax.ShapeDtypeStruct((b, h, s // vb, D_V, vb), BF16),
        jax.ShapeDtypeStruct((b, s, conv_dim), BF16),
        jax.ShapeDtypeStruct((b, 8, conv_dim), F32),
    )
    in_specs = [
        pl.BlockSpec((None, tm, d), lambda bi, i: (bi, i, 0)),
        pl.BlockSpec((tm, LANES), lambda bi, i: (i, 0)),
        pl.BlockSpec((tm, LANES), lambda bi, i: (i, 0)),
        pl.BlockSpec((LANES, tm), lambda bi, i: (0, i)),
        pl.BlockSpec((LANES, tm), lambda bi, i: (0, i)),
        _const_spec(uinit.shape), _const_spec(ln0g.shape), _const_spec(ln0b.shape), _const_spec(win.shape),
        _const_spec(qg.shape), _const_spec(kvg.shape), _const_spec(wqat.shape), _const_spec(wqbt.shape),
        _const_spec(wk.shape), _const_spec(wvt.shape), _const_spec(convw.shape), _const_spec(convg.shape),
    ]
    out_specs = (
        pl.BlockSpec((None, h, QK_PAD, tm), lambda bi, i: (bi, 0, 0, i)),
        pl.BlockSpec((None, h, tm, QK_PAD), lambda bi, i: (bi, 0, i, 0)),
        pl.BlockSpec((None, h, tm // vb, D_V, vb), lambda bi, i: (bi, 0, i, 0, 0)),
        pl.BlockSpec((None, tm, conv_dim), lambda bi, i: (bi, i, 0)),
        pl.BlockSpec((None, 8, conv_dim), lambda bi, i: (bi, 0, 0)),
    )
    return pl.pallas_call(
        functools.partial(_inproj_kernel, tm=tm, vb=vb, conv_dim=conv_dim),
        grid=grid, in_specs=in_specs, out_specs=out_specs, out_shape=out_shape,
        scratch_shapes=[pltpu.VMEM((tm + 8, conv_dim), F32)],
        compiler_params=pltpu.CompilerParams(
            dimension_semantics=("arbitrary", "arbitrary"), vmem_limit_bytes=VMEM_LIMIT),
        name="inproj",
    )(x, t1, t2, t1t, t2t, uinit, ln0g, ln0b, win, qg, kvg, wqat, wqbt, wk, wvt, convw, convg)


def _attn_kernel(qt_ref, k_ref, vt_ref, km_ref, vmt_ref, o_ref, acc_ref, sa_ref, sb_ref, *, tq):
    i = pl.program_id(2)
    kb, qb = _KV_BLOCK, _Q_BLOCK
    ncb = tq // qb
    nd = tq // kb
    assert nd == 2, "the two-buffer pipeline consumes key blocks in pairs"

    def cols(cb):
        return slice(cb * qb, (cb + 1) * qb)

    def with_ones(vt):
        return jnp.concatenate([vt, jnp.ones((_DENOM_ROWS, vt.shape[1]), BF16)], axis=0)

    def absorb(cb, m, s, smax, vt1):
        mn = jnp.maximum(m, smax)
        a = jnp.exp2(m - mn)
        p = jnp.exp2(s - mn)
        acc_ref[:, cols(cb)] = a * acc_ref[:, cols(cb)] + _dot(vt1, p.astype(BF16))
        return mn

    def stash(s_ref, k, cb):
        s = _dot(k, qt_ref[:, cols(cb)])
        s_ref[:, cols(cb)] = s
        return jnp.max(s, axis=0, keepdims=True)

    def key_block(j):
        return k_ref[pl.ds(pl.multiple_of(j * kb, kb), kb), :]

    def trade(carry, s_old, smax_old, vt_old, s_new, j_new):
        k_new = key_block(j_new)
        vt1 = with_ones(vt_old)
        ms, smax_new = list(carry), []
        for cb in range(ncb):
            s = s_old[:, cols(cb)]
            smax_new.append(stash(s_new, k_new, cb))
            ms[cb] = absorb(cb, ms[cb], s, smax_old[cb], vt1)
        return tuple(ms), tuple(smax_new)

    k0 = key_block(0)
    max_a = tuple(stash(sa_ref, k0, cb) for cb in range(ncb))

    ms = []
    meta_rows = lax.broadcasted_iota(jnp.int32, (km_ref.shape[0], qb), 0) < N_META
    vmt1 = with_ones(vmt_ref[...])
    for cb in range(ncb):
        s = jnp.where(meta_rows, _dot(km_ref[...], qt_ref[:, cols(cb)]), NEG_INF)
        m0 = jnp.max(s, axis=0, keepdims=True)
        ms.append(m0)
        acc_ref[:, cols(cb)] = _dot(vmt1, jnp.exp2(s - m0).astype(BF16))

    def pair(j, carry):
        stats, max_a = carry
        stats, max_b = trade(stats, sa_ref, max_a, vt_ref[j], sb_ref, j + 1)
        return trade(stats, sb_ref, max_b, vt_ref[j + 1], sa_ref, j + 2)

    def pairs(t, c):
        for u in range(_PAIRS_PER_TRIP):
            c = pair(2 * (_PAIRS_PER_TRIP * t + u), c)
        return c

    whole = i // _PAIRS_PER_TRIP
    carry = lax.fori_loop(0, whole, pairs, (tuple(ms), max_a))
    ms, _ = lax.fori_loop(whole * _PAIRS_PER_TRIP, i, lambda jp, c: pair(2 * jp, c), carry)
    ms = list(ms)

    def masked(s, jj, cb):
        if (jj + 1) * kb <= cb * qb + CHUNK:
            return s
        key_chunk = (lax.broadcasted_iota(jnp.int32, (kb, qb), 0) + jj * kb) >> _CHUNK_SHIFT
        qry_chunk = (lax.broadcasted_iota(jnp.int32, (kb, qb), 1) + cb * qb) >> _CHUNK_SHIFT
        return jnp.where(key_chunk <= qry_chunk, s, NEG_INF)

    j0 = i * nd
    k_last = key_block(j0 + 1)
    second_from = kb // qb
    vt1 = with_ones(vt_ref[j0])
    for cb in range(ncb):
        s = masked(sa_ref[:, cols(cb)], 0, cb)
        if cb >= second_from:
            sb_ref[:, cols(cb)] = _dot(k_last, qt_ref[:, cols(cb)])
        ms[cb] = absorb(cb, ms[cb], s, jnp.max(s, axis=0, keepdims=True), vt1)
    vt1 = with_ones(vt_ref[j0 + 1])
    for cb in range(second_from, ncb):
        s = masked(sb_ref[:, cols(cb)], 1, cb)
        ms[cb] = absorb(cb, ms[cb], s, jnp.max(s, axis=0, keepdims=True), vt1)

    for cb in range(ncb):
        o = acc_ref[0:D_V, cols(cb)] / acc_ref[D_V:D_V + 1, cols(cb)]
        o_ref[cb * qb:(cb + 1) * qb, :] = o.T.astype(BF16)


def _attention(qt, k, vt, km, vmt, *, tq):
    b, h, _, s = qt.shape
    grid = (b, h, s // tq)
    nkb = vt.shape[2]
    return pl.pallas_call(
        functools.partial(_attn_kernel, tq=tq),
        grid=grid,
        in_specs=[
            pl.BlockSpec((None, None, QK_PAD, tq), lambda bi, hi, i: (bi, hi, 0, i)),
            pl.BlockSpec((None, None, s, QK_PAD), lambda bi, hi, i: (bi, hi, 0, 0)),
            pl.BlockSpec((None, None, nkb, D_V, _KV_BLOCK), lambda bi, hi, i: (bi, hi, 0, 0, 0)),
            pl.BlockSpec((None, LANES, QK_PAD), lambda bi, hi, i: (hi, 0, 0)),
            pl.BlockSpec((None, D_V, LANES), lambda bi, hi, i: (hi, 0, 0)),
        ],
        out_specs=pl.BlockSpec((None, tq, D_V), lambda bi, hi, i: (bi, i, hi)),
        out_shape=jax.ShapeDtypeStruct((b, s, h * D_V), BF16),
        scratch_shapes=[pltpu.VMEM((D_V + _DENOM_ROWS, tq), F32), pltpu.VMEM((_KV_BLOCK, tq), F32),
                        pltpu.VMEM((_KV_BLOCK, tq), F32)],
        compiler_params=pltpu.CompilerParams(
            dimension_semantics=("arbitrary", "arbitrary", "arbitrary"), vmem_limit_bytes=VMEM_LIMIT),
        name="attn",
    )(qt, k, vt, km, vmt)


def _outproj_kernel(xscale_ref, x_ref, oa_ref, cn_ref, ln0g_ref, ln0b_ref, ag_ref, wo_ref, ln1g_ref, ln1b_ref,
                    wq_ref, sk_ref, h1_out, h1q_out, st_out):
    h0 = _layer_norm(x_ref[...], ln0g_ref[...], ln0b_ref[...])
    an = _rms_norm(oa_ref[...].astype(F32), ag_ref[...]).astype(BF16)
    mixed = _dot(an, wo_ref[0:ATTN_DIM, :]) + _dot(cn_ref[...], wo_ref[ATTN_DIM:, :])
    h1 = _layer_norm(DEEPNORM_ALPHA * h0 + mixed, ln1g_ref[...], ln1b_ref[...])
    h1_out[...] = h1
    h1q_out[...] = (h1 * xscale_ref[0]).astype(F8)
    h1b = h1.astype(BF16)
    pq = _dot(h1b, wq_ref[...])
    for hc in range(PEER_HEADS * 2):
        pqs = pq[:, hc * D_HALF:(hc + 1) * D_HALF].astype(BF16)
        st_out[hc * N_KEYS:(hc + 1) * N_KEYS, :] = _dot_nt(sk_ref[hc % 2], pqs)


def _outproj(xscale, x, oa, cn, ln0g, ln0b, ag, wo, ln1g, ln1b, wq, sk, *, tm):
    b, s, d = x.shape
    grid = (b, s // tm)
    nsc = PEER_HEADS * 2 * N_KEYS
    tok = lambda w: pl.BlockSpec((None, tm, w), lambda bi, i: (bi, i, 0))
    return pl.pallas_call(
        _outproj_kernel,
        grid=grid,
        in_specs=[pl.BlockSpec(memory_space=pltpu.SMEM), tok(d), tok(oa.shape[2]), tok(cn.shape[2]),
                  _const_spec(ln0g.shape), _const_spec(ln0b.shape), _const_spec(ag.shape), _const_spec(wo.shape),
                  _const_spec(ln1g.shape), _const_spec(ln1b.shape), _const_spec(wq.shape), _const_spec(sk.shape)],
        out_specs=(tok(d), tok(d), pl.BlockSpec((None, nsc, tm), lambda bi, i: (bi, 0, i))),
        out_shape=(jax.ShapeDtypeStruct((b, s, d), F32), jax.ShapeDtypeStruct((b, s, d), F8),
                   jax.ShapeDtypeStruct((b, nsc, s), F32)),
        compiler_params=pltpu.CompilerParams(
            dimension_semantics=("arbitrary", "arbitrary"), vmem_limit_bytes=VMEM_LIMIT),
        name="outproj",
    )(xscale, x, oa, cn, ln0g, ln0b, ag, wo, ln1g, ln1b, wq, sk)


_BIG_ID = 1 << 20


_SUBLANES = 8


def _top_keys(s_ref, base, n):
    w = s_ref.shape[1]
    nv = N_KEYS // _SUBLANES
    assert n <= nv
    sub = lax.broadcasted_iota(jnp.int32, (_SUBLANES, w), 0)
    vals = [s_ref[pl.ds(pl.multiple_of(base + r * _SUBLANES, _SUBLANES), _SUBLANES), :] for r in range(nv)]
    ids = [sub + r * _SUBLANES for r in range(nv)]
    for rnd in range(nv):
        for i in range(rnd % 2, nv - 1, 2):
            swap = vals[i + 1] > vals[i]
            hi, lo = jnp.maximum(vals[i], vals[i + 1]), jnp.minimum(vals[i], vals[i + 1])
            ids[i], ids[i + 1] = jnp.where(swap, ids[i + 1], ids[i]), jnp.where(swap, ids[i], ids[i + 1])
            vals[i], vals[i + 1] = hi, lo
    rows = lax.broadcasted_iota(jnp.int32, (n, w), 0)
    out_v = jnp.zeros((n, w), F32)
    out_i = jnp.zeros((n, w), jnp.int32)
    val_rows = []
    for t in range(n):
        m = jnp.max(vals[0], axis=0, keepdims=True)
        pick = jnp.min(jnp.where(vals[0] == m, ids[0], _BIG_ID), axis=0, keepdims=True)
        out_v = jnp.where(rows == t, m, out_v)
        out_i = jnp.where(rows == t, pick, out_i)
        val_rows.append(m)
        win = ids[0] == pick
        for r in range(n - 1 - t):
            vals[r] = jnp.where(win, vals[r + 1], vals[r])
            ids[r] = jnp.where(win, ids[r + 1], ids[r])
    return out_v, out_i, val_rows


def _top_pair_sums(v1, v2rows, n):
    w = v1.shape[1]
    half = n // 2
    assert half == _SUBLANES
    ra = lax.broadcasted_iota(jnp.int32, (half, w), 0)
    top = v1[0:half, :]
    bot = v1[half:, :] + v2rows[0]
    bot_id = (ra + half) * n
    vals = [jnp.where(ra < n // (b + 1), top + v2rows[b], -jnp.inf) for b in range(n)]
    ids = [ra * n + b for b in range(n)]
    rows = lax.broadcasted_iota(jnp.int32, (n, w), 0)
    out_v = jnp.zeros((n, w), F32)
    out_i = jnp.zeros((n, w), jnp.int32)
    for t in range(n):
        m = jnp.max(jnp.maximum(vals[0], bot), axis=0, keepdims=True)
        lowest = jnp.minimum(jnp.where(vals[0] == m, ids[0], _BIG_ID), jnp.where(bot == m, bot_id, _BIG_ID))
        pick = jnp.min(lowest, axis=0, keepdims=True)
        out_v = jnp.where(rows == t, m, out_v)
        out_i = jnp.where(rows == t, pick, out_i)
        bot = jnp.where(bot_id == pick, -jnp.inf, bot)
        win = ids[0] == pick
        for b in range(n - 1 - t):
            vals[b] = jnp.where(win, vals[b + 1], vals[b])
            ids[b] = jnp.where(win, ids[b + 1], ids[b])
    return out_v, out_i


def _route_kernel(st_ref, i1_out, i2_out, g_out):
    w = st_ref.shape[1]
    k = PEER_TOPK

    def head(h, _):
        base = pl.multiple_of(h * 2 * N_KEYS, 2 * N_KEYS)
        v1, i1, _ = _top_keys(st_ref, base, k)
        _, i2, v2rows = _top_keys(st_ref, base + N_KEYS, k)
        best, best_id = _top_pair_sums(v1, v2rows, k)
        a_sel = best_id >> _TOPK_SHIFT
        b_sel = best_id & (k - 1)
        e1 = jnp.zeros((k, w), jnp.int32)
        e2 = jnp.zeros((k, w), jnp.int32)
        for c in range(k):
            e1 = jnp.where(a_sel == c, i1[c:c + 1, :], e1)
            e2 = jnp.where(b_sel == c, i2[c:c + 1, :], e2)
        ex = jnp.exp(best - best[0:1, :])
        gate = ex / jnp.sum(ex, axis=0, keepdims=True)
        o = pl.multiple_of(h * k, k)
        i1_out[pl.ds(o, k), :] = e1
        i2_out[pl.ds(o, k), :] = e2
        g_out[pl.ds(o, k), :] = gate
        return 0

    lax.fori_loop(0, PEER_HEADS, head, 0, unroll=True)


def _route(st, *, tl):
    b, nsc, s = st.shape
    nj = PEER_HEADS * PEER_TOPK
    grid = (b, s // tl)
    ospec = pl.BlockSpec((None, nj, tl), lambda bi, i: (bi, 0, i))
    return pl.pallas_call(
        _route_kernel,
        grid=grid,
        in_specs=[pl.BlockSpec((None, nsc, tl), lambda bi, i: (bi, 0, i))],
        out_specs=(ospec, ospec, ospec),
        out_shape=(jax.ShapeDtypeStruct((b, nj, s), jnp.int32), jax.ShapeDtypeStruct((b, nj, s), jnp.int32),
                   jax.ShapeDtypeStruct((b, nj, s), F32)),
        compiler_params=pltpu.CompilerParams(dimension_semantics=("arbitrary", "arbitrary")),
        name="route",
    )(st)


_GROUP = 16
_PITCH = N_KEYS + 4


def _scatter_kernel(i1_ref, i2_ref, g_ref, gd_out, stage_a, stage_b):
    tg = i1_ref.shape[0]
    nj = i1_ref.shape[1]
    sub = lax.broadcasted_iota(jnp.int32, (N_KEYS, nj), 0)
    stages = (stage_a, stage_b)

    def fill(grp):
        stage = stages[grp % 2]
        for t in range(_GROUP):
            tok = grp * _GROUP + t
            r1 = i1_ref[tok:tok + 1, :]
            r2 = i2_ref[tok:tok + 1, :]
            gg = g_ref[tok:tok + 1, :]
            p1 = jnp.where(sub == r1, gg, 0.0).astype(BF16)
            p2 = jnp.where(sub == r2, 1.0, 0.0).astype(BF16)
            stage[t * _PITCH:t * _PITCH + N_KEYS, :] = _dot_nt(p1, p2)

    def drain(grp):
        stage = stages[grp % 2]
        for a in range(N_KEYS):
            rows = stage[pl.ds(a, _GROUP, stride=_PITCH), :]
            gd_out[grp * _GROUP:(grp + 1) * _GROUP, a * N_KEYS:(a + 1) * N_KEYS] = rows.astype(BF16)

    ngroups = tg // _GROUP
    fill(0)
    for grp in range(ngroups):
        if grp + 1 < ngroups:
            fill(grp + 1)
        drain(grp)


def _scatter(i1, i2, g, *, tg):
    t, nj = i1.shape
    ne = N_KEYS * N_KEYS
    spec = pl.BlockSpec((tg, nj), lambda i: (i, 0))
    return pl.pallas_call(
        _scatter_kernel,
        grid=(t // tg,),
        in_specs=[spec, spec, spec],
        out_specs=pl.BlockSpec((tg, ne), lambda i: (i, 0)),
        out_shape=jax.ShapeDtypeStruct((t, ne), BF16),
        scratch_shapes=[pltpu.VMEM((_GROUP * _PITCH, N_KEYS), F32), pltpu.VMEM((_GROUP * _PITCH, N_KEYS), F32)],
        compiler_params=pltpu.CompilerParams(dimension_semantics=("arbitrary",)),
        name="scatter",
    )(i1, i2, g)


def _peer_kernel(unscale_ref, xq_ref, h1_ref, u_ref, v_ref, gd_ref, ln2g_ref, ln2b_ref, o_ref):
    e = pl.program_id(1)

    @pl.when(e == 0)
    def _():
        o_ref[...] = jnp.zeros_like(o_ref)

    raw = _dot_nt(xq_ref[...], u_ref[...])
    w = 1.0 + lax.erf(raw * (unscale_ref[0] * 2.0 ** -0.5))
    hd = (raw * w * gd_ref[...].astype(F32)).astype(F8)
    o_ref[...] += _dot(hd, v_ref[...])

    @pl.when(e == pl.num_programs(1) - 1)
    def _():
        f = o_ref[...] * unscale_ref[1]
        o_ref[...] = _layer_norm(DEEPNORM_ALPHA * h1_ref[...] + f, ln2g_ref[...], ln2b_ref[...])


def _peer(unscale, xq, h1, u, v, gd, ln2g, ln2b, *, tm, ec):
    t, d = xq.shape
    ne = v.shape[0]
    grid = (t // tm, ne // ec)
    return pl.pallas_call(
        _peer_kernel,
        grid=grid,
        in_specs=[
            pl.BlockSpec(memory_space=pltpu.SMEM),
            pl.BlockSpec((tm, d), lambda i, e: (i, 0)),
            pl.BlockSpec((tm, d), lambda i, e: (i, 0), pipeline_mode=pl.Buffered(1)),
            pl.BlockSpec((ec, d), lambda i, e: (e, 0)),
            pl.BlockSpec((ec, d), lambda i, e: (e, 0)),
            pl.BlockSpec((tm, ec), lambda i, e: (i, e)),
            _const_spec(ln2g.shape), _const_spec(ln2b.shape),
        ],
        out_specs=pl.BlockSpec((tm, d), lambda i, e: (i, 0)),
        out_shape=jax.ShapeDtypeStruct((t, d), F32),
        compiler_params=pltpu.CompilerParams(
            dimension_semantics=("arbitrary", "arbitrary"), vmem_limit_bytes=VMEM_LIMIT),
        name="peer",
    )(unscale, xq, h1, u, v, gd, ln2g, ln2b)


def _pow2_below(x):
    return jnp.exp2(jnp.floor(jnp.log2(x)))


def _peer_scales(ln1_g, ln1_b, peer_u, peer_v, d):
    tiny = jnp.float32(1e-30)
    h_elem = jnp.max(jnp.sqrt(float(d)) * jnp.abs(ln1_g) + jnp.abs(ln1_b))
    h_norm = jnp.sqrt(float(d)) * jnp.max(jnp.abs(ln1_g)) + jnp.sqrt(jnp.sum(ln1_b * ln1_b))
    u_row = jnp.sqrt(jnp.max(jnp.sum(peer_u * peer_u, axis=1)))
    u_max = jnp.max(jnp.max(jnp.abs(peer_u), axis=1))
    sx = _pow2_below(_F8_TARGET / jnp.maximum(h_elem, tiny))
    su = _pow2_below(_F8_TARGET / jnp.maximum(u_max, tiny))
    sv = _pow2_below(_F8_TARGET / jnp.maximum(jnp.max(jnp.abs(peer_v)), tiny))
    sh = _pow2_below(_F8_TARGET / jnp.maximum(h_norm * u_row, tiny))
    return sx, su, sv, sh


def _rope_tables(first_pos, n):
    inv = 1.0 / (ROPE_THETA ** (jnp.arange(0, D_ROPE, 2, dtype=F32) / D_ROPE))
    ang = (first_pos + jnp.arange(n, dtype=F32))[:, None] * inv[None, :]
    cos, sin = jnp.cos(ang), jnp.sin(ang)
    z = jnp.zeros((n, LANES - D_ROPE), F32)
    return jnp.concatenate([cos, cos, z], axis=1), jnp.concatenate([-sin, sin, z], axis=1)


def _swap_halves(w):
    half = w.shape[-1] // 2
    return jnp.concatenate([w[..., half:], w[..., :half]], axis=-1)


def _pick(n, prefs):
    for p in prefs:
        if n % p == 0:
            return p
    raise ValueError(f"no tile in {prefs} divides {n}")


def kernel(x, meta_tokens, ln0_g, ln0_b, w_in, q_norm_g, kv_norm_g, w_uq, w_ukv, conv_w, attn_out_g, conv_out_g,
           w_o, ln1_g, ln1_b, peer_w_query, peer_sub_keys, peer_u, peer_v, ln2_g, ln2_b):
    b, s, d = x.shape
    assert w_in.shape[0] == DEPTH and s % _Q_TILE == 0 and conv_w.shape[1] == CONV_WIDTH
    conv_dim = conv_w.shape[2]
    h = MLA_HEADS
    row = lambda a: a.reshape(1, -1).astype(F32)

    wi = w_in[0]
    o_kr, o_b = Q_RANK + KV_RANK, Q_RANK + KV_RANK + D_ROPE
    w_kr = wi[:, o_kr:o_b]
    zpad = jnp.zeros((d, LANES - D_ROPE), F32)
    win = jnp.concatenate(
        [wi[:, :o_kr], w_kr, zpad, _swap_halves(w_kr), zpad, wi[:, o_b:]], axis=1).astype(BF16)
    wq3 = w_uq[0].reshape(Q_RANK, h, D_QK)
    zq = jnp.zeros((Q_RANK, h, QK_PAD - D_QK), F32)
    wqat = jnp.concatenate([wq3, zq], axis=2).reshape(Q_RANK, h * QK_PAD).T.astype(BF16)
    wqbt = jnp.concatenate([_swap_halves(wq3[:, :, D_NOPE:]), zq], axis=2).reshape(Q_RANK, h * LANES).T.astype(BF16)
    wkv3 = w_ukv[0].reshape(KV_RANK, h, D_NOPE + D_V)
    wk = wkv3[:, :, :D_NOPE].reshape(KV_RANK, h * D_NOPE).astype(BF16)
    wvt = wkv3[:, :, D_NOPE:].reshape(KV_RANK, h * D_V).T.astype(BF16)
    wo = w_o[0].astype(BF16)
    wpq = peer_w_query[0].astype(BF16)
    sk = peer_sub_keys[0].astype(BF16)
    sx, su, sv, sh = _peer_scales(ln1_g[0], ln1_b[0], peer_u[0], peer_v[0], d)
    pu = (peer_u[0] * su).astype(F8)
    pv = (peer_v[0] * sv).astype(F8)
    unscale = jnp.stack([1.0 / (sx * su), 1.0 / (sh * sv)]).astype(F32)

    inproj_w = (row(ln0_g), row(ln0_b), win, row(q_norm_g[0]), row(kv_norm_g[0]), wqat, wqbt, wk, wvt,
                conv_w[0].astype(F32), row(conv_out_g[0]))

    t1m, t2m = _rope_tables(0, N_META)
    _, km, vmt, _, utail = _inproj(meta_tokens[None].astype(F32), t1m, t2m, t1m.T, t2m.T,
                                   jnp.zeros((8, conv_dim), F32), *inproj_w, tm=N_META)
    km = jnp.pad(km[0], ((0, 0), (0, LANES - N_META), (0, 0)))
    vmt = jnp.pad(vmt[0, :, 0], ((0, 0), (0, 0), (0, LANES - N_META)))

    t1, t2 = _rope_tables(N_META, s)
    qt, k, vt, convn, _ = _inproj(x, t1, t2, t1.T, t2.T, utail[0], *inproj_w, tm=_KV_BLOCK)
    oa = _attention(qt, k, vt, km, vmt, tq=_Q_TILE)
    h1, h1q, st = _outproj(sx.reshape(1), x, oa, convn, row(ln0_g), row(ln0_b), row(attn_out_g[0]), wo,
                           row(ln1_g[0]), row(ln1_b[0]), wpq, sk, tm=_pick(s, (256, 128)))
    i1, i2, g = _route(st, tl=LANES)
    tok = lambda a: jnp.swapaxes(a, 1, 2).reshape(b * s, a.shape[1])
    gd = _scatter(tok(i1), tok(i2), tok(g) * (sh * 0.5 * unscale[0]), tg=_pick(b * s, (256, 128, 64)))
    out = _peer(unscale, h1q.reshape(b * s, d), h1.reshape(b * s, d), pu, pv, gd, row(ln2_g[0]), row(ln2_b[0]),
                tm=_pick(b * s, (1024, 512)), ec=_PEER_CHUNK)
    return out.reshape(b, s, d)
```

```python
import functools

import jax
import jax.numpy as jnp
from jax import lax
from jax.experimental import pallas as pl
from jax.experimental.pallas import tpu as pltpu

CHUNK = 64
N_META = 16
MLA_HEADS = 8
D_NOPE = 128
D_ROPE = 64
D_QK = D_NOPE + D_ROPE
D_V = 128
Q_RANK = 384
KV_RANK = 512
ROPE_THETA = 10000.0
ATTN_DIM = MLA_HEADS * D_V
CONV_WIDTH = 3
PEER_HEADS = 8
N_KEYS = 128
D_HALF = 128
PEER_TOPK = 16
DEPTH = 1
DEEPNORM_ALPHA = (2.0 * DEPTH) ** 0.25
EPS = 1e-5
NEG_INF = -1e30

_CHUNK_SHIFT = CHUNK.bit_length() - 1
_TOPK_SHIFT = PEER_TOPK.bit_length() - 1
assert 1 << _CHUNK_SHIFT == CHUNK and 1 << _TOPK_SHIFT == PEER_TOPK

LANES = 128
QK_PAD = 256
_KV_BLOCK = 512
_Q_BLOCK = 256
_Q_TILE = 2 * _KV_BLOCK
_PAIRS_PER_TRIP = 4
_DENOM_ROWS = 16
_PEER_CHUNK = 1024
_LOG2E = 1.4426950408889634
VMEM_LIMIT = 58 * 1024 * 1024

F32 = jnp.float32
BF16 = jnp.bfloat16
F8 = jnp.float8_e4m3fn
_F8_TARGET = 224.0


def _dot(a, b):
    return jnp.dot(a, b, preferred_element_type=F32)


def _dot_nt(a, b):
    return lax.dot_general(a, b, (((1,), (1,)), ((), ())), preferred_element_type=F32)


def _layer_norm(x, g, b):
    mu = jnp.mean(x, axis=-1, keepdims=True)
    xc = x - mu
    var = jnp.mean(xc * xc, axis=-1, keepdims=True)
    return xc * lax.rsqrt(var + EPS) * g + b


def _rms_norm(x, g):
    return x * lax.rsqrt(jnp.mean(x * x, axis=-1, keepdims=True) + EPS) * g


def _const_spec(shape):
    nd = len(shape)
    return pl.BlockSpec(shape, lambda *_: (0,) * nd, pipeline_mode=pl.Buffered(1))


_OFF_CQ = 0
_OFF_CKV = _OFF_CQ + Q_RANK
_OFF_KRA = _OFF_CKV + KV_RANK
_OFF_KRB = _OFF_KRA + LANES
_OFF_BG = _OFF_KRB + LANES


def _inproj_kernel(x_ref, t1_ref, t2_ref, t1t_ref, t2t_ref, uinit_ref, ln0g_ref, ln0b_ref, win_ref, qg_ref,
                   kvg_ref, wqat_ref, wqbt_ref, wk_ref, wvt_ref, convw_ref, convg_ref,
                   qt_out, k_out, vt_out, convn_out, utail_out, ubuf, *, tm, vb, conv_dim):
    i = pl.program_id(1)
    off_cg = _OFF_BG + conv_dim
    off_hc = off_cg + conv_dim

    @pl.when(i == 0)
    def _():
        ubuf[0:8, :] = uinit_ref[...]

    hb = _layer_norm(x_ref[...], ln0g_ref[...], ln0b_ref[...]).astype(BF16)
    t1 = t1_ref[...]
    t2 = t2_ref[...]
    t1t = t1t_ref[...]
    t2t = t2t_ref[...]
    qscale = D_QK ** -0.5 * _LOG2E

    cq = _dot(hb, win_ref[:, _OFF_CQ:_OFF_CQ + Q_RANK])
    ckv = _dot(hb, win_ref[:, _OFF_CKV:_OFF_CKV + KV_RANK])
    kra = _dot(hb, win_ref[:, _OFF_KRA:_OFF_KRA + LANES])
    krb = _dot(hb, win_ref[:, _OFF_KRB:_OFF_KRB + LANES])
    krot = (kra * t1 + krb * t2).astype(BF16)
    cqn = _rms_norm(cq, qg_ref[...]).astype(BF16)
    ckvn = _rms_norm(ckv, kvg_ref[...]).astype(BF16)

    qat_all = _dot_nt(wqat_ref[...], cqn)
    qbt_all = _dot_nt(wqbt_ref[...], cqn)
    vt_all = _dot_nt(wvt_ref[...], ckvn).astype(BF16)
    kn_all = _dot(ckvn, wk_ref[...]).astype(BF16)
    for h in range(MLA_HEADS):
        qat = qat_all[h * QK_PAD:(h + 1) * QK_PAD, :]
        qbt = qbt_all[h * LANES:(h + 1) * LANES, :]
        qt_out[h, 0:LANES, :] = (qat[0:LANES, :] * qscale).astype(BF16)
        qt_out[h, LANES:QK_PAD, :] = ((qat[LANES:QK_PAD, :] * t1t + qbt * t2t) * qscale).astype(BF16)
        k_out[h, :, 0:LANES] = kn_all[:, h * LANES:(h + 1) * LANES]
        k_out[h, :, LANES:QK_PAD] = krot
        vt = vt_all[h * D_V:(h + 1) * D_V, :]
        for bk in range(tm // vb):
            vt_out[h, bk] = vt[:, bk * vb:(bk + 1) * vb]

    bg = _dot(hb, win_ref[:, _OFF_BG:_OFF_BG + conv_dim])
    cg = _dot(hb, win_ref[:, off_cg:off_cg + conv_dim])
    hc = _dot(hb, win_ref[:, off_hc:off_hc + conv_dim])
    u = cg * hc
    ubuf[8:8 + tm, :] = u
    u1 = ubuf[7:7 + tm, :]
    u2 = ubuf[6:6 + tm, :]
    w = convw_ref[...]
    y = bg * (u2 * w[0:1, :] + u1 * w[1:2, :] + u * w[2:3, :])
    convn_out[...] = _rms_norm(y, convg_ref[...]).astype(BF16)
    tail = ubuf[tm:tm + 8, :]
    utail_out[...] = tail
    ubuf[0:8, :] = tail


def _inproj(x, t1, t2, t1t, t2t, uinit, ln0g, ln0b, win, qg, kvg, wqat, wqbt, wk, wvt, convw, convg, *, tm):
    b, s, d = x.shape
    conv_dim = convw.shape[1]
    grid = (b, s // tm)
    h = MLA_HEADS
    vb = min(tm, _KV_BLOCK)
    out_shape = (
        jax.ShapeDtypeStruct((b, h, QK_PAD, s), BF16),
        jax.ShapeDtypeStruct((b, h, s, QK_PAD), BF16),
        jax.ShapeDtypeStruct((b, h, s // vb, D_V, vb), BF16),
        jax.ShapeDtypeStruct((b, s, conv_dim), BF16),
        jax.ShapeDtypeStruct((b, 8, conv_dim), F32),
    )
    in_specs = [
        pl.BlockSpec((None, tm, d), lambda bi, i: (bi, i, 0)),
        pl.BlockSpec((tm, LANES), lambda bi, i: (i, 0)),
        pl.BlockSpec((tm, LANES), lambda bi, i: (i, 0)),
        pl.BlockSpec((LANES, tm), lambda bi, i: (0, i)),
        pl.BlockSpec((LANES, tm), lambda bi, i: (0, i)),
        _const_spec(uinit.shape), _const_spec(ln0g.shape), _const_spec(ln0b.shape), _const_spec(win.shape),
        _const_spec(qg.shape), _const_spec(kvg.shape), _const_spec(wqat.shape), _const_spec(wqbt.shape),
        _const_spec(wk.shape), _const_spec(wvt.shape), _const_spec(convw.shape), _const_spec(convg.shape),
    ]
    out_specs = (
        pl.BlockSpec((None, h, QK_PAD, tm), lambda bi, i: (bi, 0, 0, i)),
        pl.BlockSpec((None, h, tm, QK_PAD), lambda bi, i: (bi, 0, i, 0)),
        pl.BlockSpec((None, h, tm // vb, D_V, vb), lambda bi, i: (bi, 0, i, 0, 0)),
        pl.BlockSpec((None, tm, conv_dim), lambda bi, i: (bi, i, 0)),
        pl.BlockSpec((None, 8, conv_dim), lambda bi, i: (bi, 0, 0)),
    )
    return pl.pallas_call(
        functools.partial(_inproj_kernel, tm=tm, vb=vb, conv_dim=conv_dim),
        grid=grid, in_specs=in_specs, out_specs=out_specs, out_shape=out_shape,
        scratch_shapes=[pltpu.VMEM((tm + 8, conv_dim), F32)],
        compiler_params=pltpu.CompilerParams(
            dimension_semantics=("arbitrary", "arbitrary"), vmem_limit_bytes=VMEM_LIMIT),
        name="inproj",
    )(x, t1, t2, t1t, t2t, uinit, ln0g, ln0b, win, qg, kvg, wqat, wqbt, wk, wvt, convw, convg)


def _attn_kernel(qt_ref, k_ref, vt_ref, km_ref, vmt_ref, o_ref, acc_ref, sa_ref, sb_ref, *, tq):
    i = pl.program_id(2)
    kb, qb = _KV_BLOCK, _Q_BLOCK
    ncb = tq // qb
    nd = tq // kb
    assert nd == 2, "the two-buffer pipeline consumes key blocks in pairs"

    def cols(cb):
        return slice(cb * qb, (cb + 1) * qb)

    def with_ones(vt):
        return jnp.concatenate([vt, jnp.ones((_DENOM_ROWS, vt.shape[1]), BF16)], axis=0)

    def absorb(cb, m, s, smax, vt1):
        mn = jnp.maximum(m, smax)
        a = jnp.exp2(m - mn)
        p = jnp.exp2(s - mn)
        acc_ref[:, cols(cb)] = a * acc_ref[:, cols(cb)] + _dot(vt1, p.astype(BF16))
        return mn

    def stash(s_ref, k, cb):
        s = _dot(k, qt_ref[:, cols(cb)])
        s_ref[:, cols(cb)] = s
        return jnp.max(s, axis=0, keepdims=True)

    def key_block(j):
        return k_ref[pl.ds(pl.multiple_of(j * kb, kb), kb), :]

    def trade(carry, s_old, smax_old, vt_old, s_new, j_new):
        k_new = key_block(j_new)
        vt1 = with_ones(vt_old)
        ms, smax_new = list(carry), []
        for cb in range(ncb):
            s = s_old[:, cols(cb)]
            smax_new.append(stash(s_new, k_new, cb))
            ms[cb] = absorb(cb, ms[cb], s, smax_old[cb], vt1)
        return tuple(ms), tuple(smax_new)

    k0 = key_block(0)
    max_a = tuple(stash(sa_ref, k0, cb) for cb in range(ncb))

    ms = []
    meta_rows = lax.broadcasted_iota(jnp.int32, (km_ref.shape[0], qb), 0) < N_META
    vmt1 = with_ones(vmt_ref[...])
    for cb in range(ncb):
        s = jnp.where(meta_rows, _dot(km_ref[...], qt_ref[:, cols(cb)]), NEG_INF)
        m0 = jnp.max(s, axis=0, keepdims=True)
        ms.append(m0)
        acc_ref[:, cols(cb)] = _dot(vmt1, jnp.exp2(s - m0).astype(BF16))

    def pair(j, carry):
        stats, max_a = carry
        stats, max_b = trade(stats, sa_ref, max_a, vt_ref[j], sb_ref, j + 1)
        return trade(stats, sb_ref, max_b, vt_ref[j + 1], sa_ref, j + 2)

    def pairs(t, c):
        for u in range(_PAIRS_PER_TRIP):
            c = pair(2 * (_PAIRS_PER_TRIP * t + u), c)
        return c

    whole = i // _PAIRS_PER_TRIP
    carry = lax.fori_loop(0, whole, pairs, (tuple(ms), max_a))
    ms, _ = lax.fori_loop(whole * _PAIRS_PER_TRIP, i, lambda jp, c: pair(2 * jp, c), carry)
    ms = list(ms)

    def masked(s, jj, cb):
        if (jj + 1) * kb <= cb * qb + CHUNK:
            return s
        key_chunk = (lax.broadcasted_iota(jnp.int32, (kb, qb), 0) + jj * kb) >> _CHUNK_SHIFT
        qry_chunk = (lax.broadcasted_iota(jnp.int32, (kb, qb), 1) + cb * qb) >> _CHUNK_SHIFT
        return jnp.where(key_chunk <= qry_chunk, s, NEG_INF)

    j0 = i * nd
    k_last = key_block(j0 + 1)
    second_from = kb // qb
    vt1 = with_ones(vt_ref[j0])
    for cb in range(ncb):
        s = masked(sa_ref[:, cols(cb)], 0, cb)
        if cb >= second_from:
            sb_ref[:, cols(cb)] = _dot(k_last, qt_ref[:, cols(cb)])
        ms[cb] = absorb(cb, ms[cb], s, jnp.max(s, axis=0, keepdims=True), vt1)
    vt1 = with_ones(vt_ref[j0 + 1])
    for cb in range(second_from, ncb):
        s = masked(sb_ref[:, cols(cb)], 1, cb)
        ms[cb] = absorb(cb, ms[cb], s, jnp.max(s, axis=0, keepdims=True), vt1)

    for cb in range(ncb):
        o = acc_ref[0:D_V, cols(cb)] / acc_ref[D_V:D_V + 1, cols(cb)]
        o_ref[cb * qb:(cb + 1) * qb, :] = o.T.astype(BF16)


def _attention(qt, k, vt, km, vmt, *, tq):
    b, h, _, s = qt.shape
    grid = (b, h, s // tq)
    nkb = vt.shape[2]
    return pl.pallas_call(
        functools.partial(_attn_kernel, tq=tq),
        grid=grid,
        in_specs=[
            pl.BlockSpec((None, None, QK_PAD, tq), lambda bi, hi, i: (bi, hi, 0, i)),
            pl.BlockSpec((None, None, s, QK_PAD), lambda bi, hi, i: (bi, hi, 0, 0)),
            pl.BlockSpec((None, None, nkb, D_V, _KV_BLOCK), lambda bi, hi, i: (bi, hi, 0, 0, 0)),
            pl.BlockSpec((None, LANES, QK_PAD), lambda bi, hi, i: (hi, 0, 0)),
            pl.BlockSpec((None, D_V, LANES), lambda bi, hi, i: (hi, 0, 0)),
        ],
        out_specs=pl.BlockSpec((None, tq, D_V), lambda bi, hi, i: (bi, i, hi)),
        out_shape=jax.ShapeDtypeStruct((b, s, h * D_V), BF16),
        scratch_shapes=[pltpu.VMEM((D_V + _DENOM_ROWS, tq), F32), pltpu.VMEM((_KV_BLOCK, tq), F32),
                        pltpu.VMEM((_KV_BLOCK, tq), F32)],
        compiler_params=pltpu.CompilerParams(
            dimension_semantics=("arbitrary", "arbitrary", "arbitrary"), vmem_limit_bytes=VMEM_LIMIT),
        name="attn",
    )(qt, k, vt, km, vmt)


def _outproj_kernel(xscale_ref, x_ref, oa_ref, cn_ref, ln0g_ref, ln0b_ref, ag_ref, wo_ref, ln1g_ref, ln1b_ref,
                    wq_ref, sk_ref, h1_out, h1q_out, st_out):
    h0 = _layer_norm(x_ref[...], ln0g_ref[...], ln0b_ref[...])
    an = _rms_norm(oa_ref[...].astype(F32), ag_ref[...]).astype(BF16)
    mixed = _dot(an, wo_ref[0:ATTN_DIM, :]) + _dot(cn_ref[...], wo_ref[ATTN_DIM:, :])
    h1 = _layer_norm(DEEPNORM_ALPHA * h0 + mixed, ln1g_ref[...], ln1b_ref[...])
    h1_out[...] = h1
    h1q_out[...] = (h1 * xscale_ref[0]).astype(F8)
    h1b = h1.astype(BF16)
    pq = _dot(h1b, wq_ref[...])
    for hc in range(PEER_HEADS * 2):
        pqs = pq[:, hc * D_HALF:(hc + 1) * D_HALF].astype(BF16)
        st_out[hc * N_KEYS:(hc + 1) * N_KEYS, :] = _dot_nt(sk_ref[hc % 2], pqs)


def _outproj(xscale, x, oa, cn, ln0g, ln0b, ag, wo, ln1g, ln1b, wq, sk, *, tm):
    b, s, d = x.shape
    grid = (b, s // tm)
    nsc = PEER_HEADS * 2 * N_KEYS
    tok = lambda w: pl.BlockSpec((None, tm, w), lambda bi, i: (bi, i, 0))
    return pl.pallas_call(
        _outproj_kernel,
        grid=grid,
        in_specs=[pl.BlockSpec(memory_space=pltpu.SMEM), tok(d), tok(oa.shape[2]), tok(cn.shape[2]),
                  _const_spec(ln0g.shape), _const_spec(ln0b.shape), _const_spec(ag.shape), _const_spec(wo.shape),
                  _const_spec(ln1g.shape), _const_spec(ln1b.shape), _const_spec(wq.shape), _const_spec(sk.shape)],
        out_specs=(tok(d), tok(d), pl.BlockSpec((None, nsc, tm), lambda bi, i: (bi, 0, i))),
        out_shape=(jax.ShapeDtypeStruct((b, s, d), F32), jax.ShapeDtypeStruct((b, s, d), F8),
                   jax.ShapeDtypeStruct((b, nsc, s), F32)),
        compiler_params=pltpu.CompilerParams(
            dimension_semantics=("arbitrary", "arbitrary"), vmem_limit_bytes=VMEM_LIMIT),
        name="outproj",
    )(xscale, x, oa, cn, ln0g, ln0b, ag, wo, ln1g, ln1b, wq, sk)


_BIG_ID = 1 << 20


_SUBLANES = 8


def _top_keys(s_ref, base, n):
    w = s_ref.shape[1]
    nv = N_KEYS // _SUBLANES
    assert n <= nv
    sub = lax.broadcasted_iota(jnp.int32, (_SUBLANES, w), 0)
    vals = [s_ref[pl.ds(pl.multiple_of(base + r * _SUBLANES, _SUBLANES), _SUBLANES), :] for r in range(nv)]
    ids = [sub + r * _SUBLANES for r in range(nv)]
    for rnd in range(nv):
        for i in range(rnd % 2, nv - 1, 2):
            swap = vals[i + 1] > vals[i]
            hi, lo = jnp.maximum(vals[i], vals[i + 1]), jnp.minimum(vals[i], vals[i + 1])
            ids[i], ids[i + 1] = jnp.where(swap, ids[i + 1], ids[i]), jnp.where(swap, ids[i], ids[i + 1])
            vals[i], vals[i + 1] = hi, lo
    rows = lax.broadcasted_iota(jnp.int32, (n, w), 0)
    out_v = jnp.zeros((n, w), F32)
    out_i = jnp.zeros((n, w), jnp.int32)
    val_rows = []
    for t in range(n):
        m = jnp.max(vals[0], axis=0, keepdims=True)
        pick = jnp.min(jnp.where(vals[0] == m, ids[0], _BIG_ID), axis=0, keepdims=True)
        out_v = jnp.where(rows == t, m, out_v)
        out_i = jnp.where(rows == t, pick, out_i)
        val_rows.append(m)
        win = ids[0] == pick
        for r in range(n - 1 - t):
            vals[r] = jnp.where(win, vals[r + 1], vals[r])
            ids[r] = jnp.where(win, ids[r + 1], ids[r])
    return out_v, out_i, val_rows


def _top_pair_sums(v1, v2rows, n):
    w = v1.shape[1]
    half = n // 2
    assert half == _SUBLANES
    ra = lax.broadcasted_iota(jnp.int32, (half, w), 0)
    top = v1[0:half, :]
    bot = v1[half:, :] + v2rows[0]
    bot_id = (ra + half) * n
    vals = [jnp.where(ra < n // (b + 1), top + v2rows[b], -jnp.inf) for b in range(n)]
    ids = [ra * n + b for b in range(n)]
    rows = lax.broadcasted_iota(jnp.int32, (n, w), 0)
    out_v = jnp.zeros((n, w), F32)
    out_i = jnp.zeros((n, w), jnp.int32)
    for t in range(n):
        m = jnp.max(jnp.maximum(vals[0], bot), axis=0, keepdims=True)
        lowest = jnp.minimum(jnp.where(vals[0] == m, ids[0], _BIG_ID), jnp.where(bot == m, bot_id, _BIG_ID))
        pick = jnp.min(lowest, axis=0, keepdims=True)
        out_v = jnp.where(rows == t, m, out_v)
        out_i = jnp.where(rows == t, pick, out_i)
        bot = jnp.where(bot_id == pick, -jnp.inf, bot)
        win = ids[0] == pick
        for b in range(n - 1 - t):
            vals[b] = jnp.where(win, vals[b + 1], vals[b])
            ids[b] = jnp.where(win, ids[b + 1], ids[b])
    return out_v, out_i


def _route_kernel(st_ref, i1_out, i2_out, g_out):
    w = st_ref.shape[1]
    k = PEER_TOPK

    def head(h, _):
        base = pl.multiple_of(h * 2 * N_KEYS, 2 * N_KEYS)
        v1, i1, _ = _top_keys(st_ref, base, k)
        _, i2, v2rows = _top_keys(st_ref, base + N_KEYS, k)
        best, best_id = _top_pair_sums(v1, v2rows, k)
        a_sel = best_id >> _TOPK_SHIFT
        b_sel = best_id & (k - 1)
        e1 = jnp.zeros((k, w), jnp.int32)
        e2 = jnp.zeros((k, w), jnp.int32)
        for c in range(k):
            e1 = jnp.where(a_sel == c, i1[c:c + 1, :], e1)
            e2 = jnp.where(b_sel == c, i2[c:c + 1, :], e2)
        ex = jnp.exp(best - best[0:1, :])
        gate = ex / jnp.sum(ex, axis=0, keepdims=True)
        o = pl.multiple_of(h * k, k)
        i1_out[pl.ds(o, k), :] = e1
        i2_out[pl.ds(o, k), :] = e2
        g_out[pl.ds(o, k), :] = gate
        return 0

    lax.fori_loop(0, PEER_HEADS, head, 0, unroll=True)


def _route(st, *, tl):
    b, nsc, s = st.shape
    nj = PEER_HEADS * PEER_TOPK
    grid = (b, s // tl)
    ospec = pl.BlockSpec((None, nj, tl), lambda bi, i: (bi, 0, i))
    return pl.pallas_call(
        _route_kernel,
        grid=grid,
        in_specs=[pl.BlockSpec((None, nsc, tl), lambda bi, i: (bi, 0, i))],
        out_specs=(ospec, ospec, ospec),
        out_shape=(jax.ShapeDtypeStruct((b, nj, s), jnp.int32), jax.ShapeDtypeStruct((b, nj, s), jnp.int32),
                   jax.ShapeDtypeStruct((b, nj, s), F32)),
        compiler_params=pltpu.CompilerParams(dimension_semantics=("arbitrary", "arbitrary")),
        name="route",
    )(st)


_GROUP = 16
_PITCH = N_KEYS + 4


def _scatter_kernel(i1_ref, i2_ref, g_ref, gd_out, stage_a, stage_b):
    tg = i1_ref.shape[0]
    nj = i1_ref.shape[1]
    sub = lax.broadcasted_iota(jnp.int32, (N_KEYS, nj), 0)
    stages = (stage_a, stage_b)

    def fill(grp):
        stage = stages[grp % 2]
        for t in range(_GROUP):
            tok = grp * _GROUP + t
            r1 = i1_ref[tok:tok + 1, :]
            r2 = i2_ref[tok:tok + 1, :]
            gg = g_ref[tok:tok + 1, :]
            p1 = jnp.where(sub == r1, gg, 0.0).astype(BF16)
            p2 = jnp.where(sub == r2, 1.0, 0.0).astype(BF16)
            stage[t * _PITCH:t * _PITCH + N_KEYS, :] = _dot_nt(p1, p2)

    def drain(grp):
        stage = stages[grp % 2]
        for a in range(N_KEYS):
            rows = stage[pl.ds(a, _GROUP, stride=_PITCH), :]
            gd_out[grp * _GROUP:(grp + 1) * _GROUP, a * N_KEYS:(a + 1) * N_KEYS] = rows.astype(BF16)

    ngroups = tg // _GROUP
    fill(0)
    for grp in range(ngroups):
        if grp + 1 < ngroups:
            fill(grp + 1)
        drain(grp)


def _scatter(i1, i2, g, *, tg):
    t, nj = i1.shape
    ne = N_KEYS * N_KEYS
    spec = pl.BlockSpec((tg, nj), lambda i: (i, 0))
    return pl.pallas_call(
        _scatter_kernel,
        grid=(t // tg,),
        in_specs=[spec, spec, spec],
        out_specs=pl.BlockSpec((tg, ne), lambda i: (i, 0)),
        out_shape=jax.ShapeDtypeStruct((t, ne), BF16),
        scratch_shapes=[pltpu.VMEM((_GROUP * _PITCH, N_KEYS), F32), pltpu.VMEM((_GROUP * _PITCH, N_KEYS), F32)],
        compiler_params=pltpu.CompilerParams(dimension_semantics=("arbitrary",)),
        name="scatter",
    )(i1, i2, g)


def _peer_kernel(unscale_ref, xq_ref, h1_ref, u_ref, v_ref, gd_ref, ln2g_ref, ln2b_ref, o_ref):
    e = pl.program_id(1)

    @pl.when(e == 0)
    def _():
        o_ref[...] = jnp.zeros_like(o_ref)

    raw = _dot_nt(xq_ref[...], u_ref[...])
    w = 1.0 + lax.erf(raw * (unscale_ref[0] * 2.0 ** -0.5))
    hd = (raw * w * gd_ref[...].astype(F32)).astype(F8)
    o_ref[...] += _dot(hd, v_ref[...])

    @pl.when(e == pl.num_programs(1) - 1)
    def _():
        f = o_ref[...] * unscale_ref[1]
        o_ref[...] = _layer_norm(DEEPNORM_ALPHA * h1_ref[...] + f, ln2g_ref[...], ln2b_ref[...])


def _peer(unscale, xq, h1, u, v, gd, ln2g, ln2b, *, tm, ec):
    t, d = xq.shape
    ne = v.shape[0]
    grid = (t // tm, ne // ec)
    return pl.pallas_call(
        _peer_kernel,
        grid=grid,
        in_specs=[
            pl.BlockSpec(memory_space=pltpu.SMEM),
            pl.BlockSpec((tm, d), lambda i, e: (i, 0)),
            pl.BlockSpec((tm, d), lambda i, e: (i, 0), pipeline_mode=pl.Buffered(1)),
            pl.BlockSpec((ec, d), lambda i, e: (e, 0)),
            pl.BlockSpec((ec, d), lambda i, e: (e, 0)),
            pl.BlockSpec((tm, ec), lambda i, e: (i, e)),
            _const_spec(ln2g.shape), _const_spec(ln2b.shape),
        ],
        out_specs=pl.BlockSpec((tm, d), lambda i, e: (i, 0)),
        out_shape=jax.ShapeDtypeStruct((t, d), F32),
        compiler_params=pltpu.CompilerParams(
            dimension_semantics=("arbitrary", "arbitrary"), vmem_limit_bytes=VMEM_LIMIT),
        name="peer",
    )(unscale, xq, h1, u, v, gd, ln2g, ln2b)


def _pow2_below(x):
    return jnp.exp2(jnp.floor(jnp.log2(x)))


def _peer_scales(ln1_g, ln1_b, peer_u, peer_v, d):
    tiny = jnp.float32(1e-30)
    h_elem = jnp.max(jnp.sqrt(float(d)) * jnp.abs(ln1_g) + jnp.abs(ln1_b))
    h_norm = jnp.sqrt(float(d)) * jnp.max(jnp.abs(ln1_g)) + jnp.sqrt(jnp.sum(ln1_b * ln1_b))
    u_row = jnp.sqrt(jnp.max(jnp.sum(peer_u * peer_u, axis=1)))
    u_max = jnp.max(jnp.max(jnp.abs(peer_u), axis=1))
    sx = _pow2_below(_F8_TARGET / jnp.maximum(h_elem, tiny))
    su = _pow2_below(_F8_TARGET / jnp.maximum(u_max, tiny))
    sv = _pow2_below(_F8_TARGET / jnp.maximum(jnp.max(jnp.abs(peer_v)), tiny))
    sh = _pow2_below(_F8_TARGET / jnp.maximum(h_norm * u_row, tiny))
    return sx, su, sv, sh


def _rope_tables(first_pos, n):
    inv = 1.0 / (ROPE_THETA ** (jnp.arange(0, D_ROPE, 2, dtype=F32) / D_ROPE))
    ang = (first_pos + jnp.arange(n, dtype=F32))[:, None] * inv[None, :]
    cos, sin = jnp.cos(ang), jnp.sin(ang)
    z = jnp.zeros((n, LANES - D_ROPE), F32)
    return jnp.concatenate([cos, cos, z], axis=1), jnp.concatenate([-sin, sin, z], axis=1)


def _swap_halves(w):
    half = w.shape[-1] // 2
    return jnp.concatenate([w[..., half:], w[..., :half]], axis=-1)


def _pick(n, prefs):
    for p in prefs:
        if n % p == 0:
            return p
    raise ValueError(f"no tile in {prefs} divides {n}")


def kernel(x, meta_tokens, ln0_g, ln0_b, w_in, q_norm_g, kv_norm_g, w_uq, w_ukv, conv_w, attn_out_g, conv_out_g,
           w_o, ln1_g, ln1_b, peer_w_query, peer_sub_keys, peer_u, peer_v, ln2_g, ln2_b):
    b, s, d = x.shape
    assert w_in.shape[0] == DEPTH and s % _Q_TILE == 0 and conv_w.shape[1] == CONV_WIDTH
    conv_dim = conv_w.shape[2]
    h = MLA_HEADS
    row = lambda a: a.reshape(1, -1).astype(F32)

    wi = w_in[0]
    o_kr, o_b = Q_RANK + KV_RANK, Q_RANK + KV_RANK + D_ROPE
    w_kr = wi[:, o_kr:o_b]
    zpad = jnp.zeros((d, LANES - D_ROPE), F32)
    win = jnp.concatenate(
        [wi[:, :o_kr], w_kr, zpad, _swap_halves(w_kr), zpad, wi[:, o_b:]], axis=1).astype(BF16)
    wq3 = w_uq[0].reshape(Q_RANK, h, D_QK)
    zq = jnp.zeros((Q_RANK, h, QK_PAD - D_QK), F32)
    wqat = jnp.concatenate([wq3, zq], axis=2).reshape(Q_RANK, h * QK_PAD).T.astype(BF16)
    wqbt = jnp.concatenate([_swap_halves(wq3[:, :, D_NOPE:]), zq], axis=2).reshape(Q_RANK, h * LANES).T.astype(BF16)
    wkv3 = w_ukv[0].reshape(KV_RANK, h, D_NOPE + D_V)
    wk = wkv3[:, :, :D_NOPE].reshape(KV_RANK, h * D_NOPE).astype(BF16)
    wvt = wkv3[:, :, D_NOPE:].reshape(KV_RANK, h * D_V).T.astype(BF16)
    wo = w_o[0].astype(BF16)
    wpq = peer_w_query[0].astype(BF16)
    sk = peer_sub_keys[0].astype(BF16)
    sx, su, sv, sh = _peer_scales(ln1_g[0], ln1_b[0], peer_u[0], peer_v[0], d)
    pu = (peer_u[0] * su).astype(F8)
    pv = (peer_v[0] * sv).astype(F8)
    unscale = jnp.stack([1.0 / (sx * su), 1.0 / (sh * sv)]).astype(F32)

    inproj_w = (row(ln0_g), row(ln0_b), win, row(q_norm_g[0]), row(kv_norm_g[0]), wqat, wqbt, wk, wvt,
                conv_w[0].astype(F32), row(conv_out_g[0]))

    t1m, t2m = _rope_tables(0, N_META)
    _, km, vmt, _, utail = _inproj(meta_tokens[None].astype(F32), t1m, t2m, t1m.T, t2m.T,
                                   jnp.zeros((8, conv_dim), F32), *inproj_w, tm=N_META)
    km = jnp.pad(km[0], ((0, 0), (0, LANES - N_META), (0, 0)))
    vmt = jnp.pad(vmt[0, :, 0], ((0, 0), (0, 0), (0, LANES - N_META)))

    t1, t2 = _rope_tables(N_META, s)
    qt, k, vt, convn, _ = _inproj(x, t1, t2, t1.T, t2.T, utail[0], *inproj_w, tm=_KV_BLOCK)
    oa = _attention(qt, k, vt, km, vmt, tq=_Q_TILE)
    h1, h1q, st = _outproj(sx.reshape(1), x, oa, convn, row(ln0_g), row(ln0_b), row(attn_out_g[0]), wo,
                           row(ln1_g[0]), row(ln1_b[0]), wpq, sk, tm=_pick(s, (256, 128)))
    i1, i2, g = _route(st, tl=LANES)
    tok = lambda a: jnp.swapaxes(a, 1, 2).reshape(b * s, a.shape[1])
    gd = _scatter(tok(i1), tok(i2), tok(g) * (sh * 0.5 * unscale[0]), tg=_pick(b * s, (256, 128, 64)))
    out = _peer(unscale, h1q.reshape(b * s, d), h1.reshape(b * s, d), pu, pv, gd, row(ln2_g[0]), row(ln2_b[0]),
                tm=_pick(b * s, (1024, 512)), ec=_PEER_CHUNK)
    return out.reshape(b, s, d)
```

```python
import functools

import jax
import jax.numpy as jnp
from jax import lax
from jax.experimental import pallas as pl
from jax.experimental.pallas import tpu as pltpu

CHUNK = 64
N_META = 16
MLA_HEADS = 8
D_NOPE = 128
D_ROPE = 64
D_QK = D_NOPE + D_ROPE
D_V = 128
Q_RANK = 384
KV_RANK = 512
ROPE_THETA = 10000.0
ATTN_DIM = MLA_HEADS * D_V
CONV_WIDTH = 3
PEER_HEADS = 8
N_KEYS = 128
D_HALF = 128
PEER_TOPK = 16
DEPTH = 1
DEEPNORM_ALPHA = (2.0 * DEPTH) ** 0.25
EPS = 1e-5
NEG_INF = -1e30

_CHUNK_SHIFT = CHUNK.bit_length() - 1
_TOPK_SHIFT = PEER_TOPK.bit_length() - 1
assert 1 << _CHUNK_SHIFT == CHUNK and 1 << _TOPK_SHIFT == PEER_TOPK

LANES = 128
QK_PAD = 256
_KV_BLOCK = 512
_Q_BLOCK = 256
_Q_TILE = 2 * _KV_BLOCK
_PAIRS_PER_TRIP = 4
_DENOM_ROWS = 16
_PEER_CHUNK = 1024
_LN_ROWS = 128
_LOG2E = 1.4426950408889634
VMEM_LIMIT = 58 * 1024 * 1024

F32 = jnp.float32
BF16 = jnp.bfloat16
F8 = jnp.float8_e4m3fn
_F8_TARGET = 224.0


def _dot(a, b):
    return jnp.dot(a, b, preferred_element_type=F32)


def _dot_nt(a, b):
    return lax.dot_general(a, b, (((1,), (1,)), ((), ())), preferred_element_type=F32)


def _layer_norm(x, g, b):
    mu = jnp.mean(x, axis=-1, keepdims=True)
    xc = x - mu
    var = jnp.mean(xc * xc, axis=-1, keepdims=True)
    return xc * lax.rsqrt(var + EPS) * g + b


def _rms_norm(x, g):
    return x * lax.rsqrt(jnp.mean(x * x, axis=-1, keepdims=True) + EPS) * g


def _const_spec(shape):
    nd = len(shape)
    return pl.BlockSpec(shape, lambda *_: (0,) * nd, pipeline_mode=pl.Buffered(1))


_OFF_CQ = 0
_OFF_CKV = _OFF_CQ + Q_RANK
_OFF_KRA = _OFF_CKV + KV_RANK
_OFF_KRB = _OFF_KRA + LANES
_OFF_BG = _OFF_KRB + LANES


def _inproj_kernel(x_ref, t1_ref, t2_ref, t1t_ref, t2t_ref, uinit_ref, ln0g_ref, ln0b_ref, win_ref, qg_ref,
                   kvg_ref, wqat_ref, wqbt_ref, wk_ref, wvt_ref, convw_ref, convg_ref,
                   qt_out, k_out, vt_out, convn_out, utail_out, ubuf, *, tm, vb, conv_dim):
    i = pl.program_id(1)
    off_cg = _OFF_BG + conv_dim
    off_hc = off_cg + conv_dim

    @pl.when(i == 0)
    def _():
        ubuf[0:8, :] = uinit_ref[...]

    hb = _layer_norm(x_ref[...], ln0g_ref[...], ln0b_ref[...]).astype(BF16)
    t1 = t1_ref[...]
    t2 = t2_ref[...]
    t1t = t1t_ref[...]
    t2t = t2t_ref[...]
    qscale = D_QK ** -0.5 * _LOG2E

    cq = _dot(hb, win_ref[:, _OFF_CQ:_OFF_CQ + Q_RANK])
    ckv = _dot(hb, win_ref[:, _OFF_CKV:_OFF_CKV + KV_RANK])
    kra = _dot(hb, win_ref[:, _OFF_KRA:_OFF_KRA + LANES])
    krb = _dot(hb, win_ref[:, _OFF_KRB:_OFF_KRB + LANES])
    krot = (kra * t1 + krb * t2).astype(BF16)
    cqn = _rms_norm(cq, qg_ref[...]).astype(BF16)
    ckvn = _rms_norm(ckv, kvg_ref[...]).astype(BF16)

    qat_all = _dot_nt(wqat_ref[...], cqn)
    qbt_all = _dot_nt(wqbt_ref[...], cqn)
    vt_all = _dot_nt(wvt_ref[...], ckvn).astype(BF16)
    kn_all = _dot(ckvn, wk_ref[...]).astype(BF16)
    for h in range(MLA_HEADS):
        qat = qat_all[h * QK_PAD:(h + 1) * QK_PAD, :]
        qbt = qbt_all[h * LANES:(h + 1) * LANES, :]
        qt_out[h, 0:LANES, :] = (qat[0:LANES, :] * qscale).astype(BF16)
        qt_out[h, LANES:QK_PAD, :] = ((qat[LANES:QK_PAD, :] * t1t + qbt * t2t) * qscale).astype(BF16)
        k_out[h, :, 0:LANES] = kn_all[:, h * LANES:(h + 1) * LANES]
        k_out[h, :, LANES:QK_PAD] = krot
        vt = vt_all[h * D_V:(h + 1) * D_V, :]
        for bk in range(tm // vb):
            vt_out[h, bk] = vt[:, bk * vb:(bk + 1) * vb]

    bg = _dot(hb, win_ref[:, _OFF_BG:_OFF_BG + conv_dim])
    cg = _dot(hb, win_ref[:, off_cg:off_cg + conv_dim])
    hc = _dot(hb, win_ref[:, off_hc:off_hc + conv_dim])
    u = cg * hc
    ubuf[8:8 + tm, :] = u
    u1 = ubuf[7:7 + tm, :]
    u2 = ubuf[6:6 + tm, :]
    w = convw_ref[...]
    y = bg * (u2 * w[0:1, :] + u1 * w[1:2, :] + u * w[2:3, :])
    convn_out[...] = _rms_norm(y, convg_ref[...]).astype(BF16)
    tail = ubuf[tm:tm + 8, :]
    utail_out[...] = tail
    ubuf[0:8, :] = tail


def _inproj(x, t1, t2, t1t, t2t, uinit, ln0g, ln0b, win, qg, kvg, wqat, wqbt, wk, wvt, convw, convg, *, tm):
    b, s, d = x.shape
    conv_dim = convw.shape[1]
    grid = (b, s // tm)
    h = MLA_HEADS
    vb = min(tm, _KV_BLOCK)
    out_shape = (
        jax.ShapeDtypeStruct((b, h, QK_PAD, s), BF16),
        jax.ShapeDtypeStruct((b, h, s, QK_PAD), BF16),
        jax.ShapeDtypeStruct((b, h, s // vb, D_V, vb), BF16),
        jax.ShapeDtypeStruct((b, s, conv_dim), BF16),
        jax.ShapeDtypeStruct((b, 8, conv_dim), F32),
    )
    in_specs = [
        pl.BlockSpec((None, tm, d), lambda bi, i: (bi, i, 0)),
        pl.BlockSpec((tm, LANES), lambda bi, i: (i, 0)),
        pl.BlockSpec((tm, LANES), lambda bi, i: (i, 0)),
        pl.BlockSpec((LANES, tm), lambda bi, i: (0, i)),
        pl.BlockSpec((LANES, tm), lambda bi, i: (0, i)),
        _const_spec(uinit.shape), _const_spec(ln0g.shape), _const_spec(ln0b.shape), _const_spec(win.shape),
        _const_spec(qg.shape), _const_spec(kvg.shape), _const_spec(wqat.shape), _const_spec(wqbt.shape),
        _const_spec(wk.shape), _const_spec(wvt.shape), _const_spec(convw.shape), _const_spec(convg.shape),
    ]
    out_specs = (
        pl.BlockSpec((None, h, QK_PAD, tm), lambda bi, i: (bi, 0, 0, i)),
        pl.BlockSpec((None, h, tm, QK_PAD), lambda bi, i: (bi, 0, i, 0)),
        pl.BlockSpec((None, h, tm // vb, D_V, vb), lambda bi, i: (bi, 0, i, 0, 0)),
        pl.BlockSpec((None, tm, conv_dim), lambda bi, i: (bi, i, 0)),
        pl.BlockSpec((None, 8, conv_dim), lambda bi, i: (bi, 0, 0)),
    )
    return pl.pallas_call(
        functools.partial(_inproj_kernel, tm=tm, vb=vb, conv_dim=conv_dim),
        grid=grid, in_specs=in_specs, out_specs=out_specs, out_shape=out_shape,
        scratch_shapes=[pltpu.VMEM((tm + 8, conv_dim), F32)],
        compiler_params=pltpu.CompilerParams(
            dimension_semantics=("arbitrary", "arbitrary"), vmem_limit_bytes=VMEM_LIMIT),
        name="inproj",
    )(x, t1, t2, t1t, t2t, uinit, ln0g, ln0b, win, qg, kvg, wqat, wqbt, wk, wvt, convw, convg)


def _attn_kernel(qt_ref, k_ref, vt_ref, km_ref, vmt_ref, o_ref, acc_ref, sa_ref, sb_ref, *, tq):
    i = pl.program_id(2)
    kb, qb = _KV_BLOCK, _Q_BLOCK
    ncb = tq // qb
    nd = tq // kb
    assert nd == 2, "the two-buffer pipeline consumes key blocks in pairs"

    def cols(cb):
        return slice(cb * qb, (cb + 1) * qb)

    def with_ones(vt):
        return jnp.concatenate([vt, jnp.ones((_DENOM_ROWS, vt.shape[1]), BF16)], axis=0)

    def absorb(cb, m, s, smax, vt1):
        mn = jnp.maximum(m, smax)
        a = jnp.exp2(m - mn)
        p = jnp.exp2(s - mn)
        acc_ref[:, cols(cb)] = a * acc_ref[:, cols(cb)] + _dot(vt1, p.astype(BF16))
        return mn

    def stash(s_ref, k, cb):
        s = _dot(k, qt_ref[:, cols(cb)])
        s_ref[:, cols(cb)] = s
        return jnp.max(s, axis=0, keepdims=True)

    def key_block(j):
        return k_ref[pl.ds(pl.multiple_of(j * kb, kb), kb), :]

    def trade(carry, s_old, smax_old, vt_old, s_new, j_new):
        k_new = key_block(j_new)
        vt1 = with_ones(vt_old)
        ms, smax_new = list(carry), []
        for cb in range(ncb):
            s = s_old[:, cols(cb)]
            smax_new.append(stash(s_new, k_new, cb))
            ms[cb] = absorb(cb, ms[cb], s, smax_old[cb], vt1)
        return tuple(ms), tuple(smax_new)

    k0 = key_block(0)
    max_a = tuple(stash(sa_ref, k0, cb) for cb in range(ncb))

    ms = []
    meta_rows = lax.broadcasted_iota(jnp.int32, (km_ref.shape[0], qb), 0) < N_META
    vmt1 = with_ones(vmt_ref[...])
    for cb in range(ncb):
        s = jnp.where(meta_rows, _dot(km_ref[...], qt_ref[:, cols(cb)]), NEG_INF)
        m0 = jnp.max(s, axis=0, keepdims=True)
        ms.append(m0)
        acc_ref[:, cols(cb)] = _dot(vmt1, jnp.exp2(s - m0).astype(BF16))

    def pair(j, carry):
        stats, max_a = carry
        stats, max_b = trade(stats, sa_ref, max_a, vt_ref[j], sb_ref, j + 1)
        return trade(stats, sb_ref, max_b, vt_ref[j + 1], sa_ref, j + 2)

    def pairs(t, c):
        for u in range(_PAIRS_PER_TRIP):
            c = pair(2 * (_PAIRS_PER_TRIP * t + u), c)
        return c

    whole = i // _PAIRS_PER_TRIP
    carry = lax.fori_loop(0, whole, pairs, (tuple(ms), max_a))
    ms, _ = lax.fori_loop(whole * _PAIRS_PER_TRIP, i, lambda jp, c: pair(2 * jp, c), carry)
    ms = list(ms)

    def masked(s, jj, cb):
        if (jj + 1) * kb <= cb * qb + CHUNK:
            return s
        key_chunk = (lax.broadcasted_iota(jnp.int32, (kb, qb), 0) + jj * kb) >> _CHUNK_SHIFT
        qry_chunk = (lax.broadcasted_iota(jnp.int32, (kb, qb), 1) + cb * qb) >> _CHUNK_SHIFT
        return jnp.where(key_chunk <= qry_chunk, s, NEG_INF)

    j0 = i * nd
    k_last = key_block(j0 + 1)
    second_from = kb // qb
    vt1 = with_ones(vt_ref[j0])
    for cb in range(ncb):
        s = masked(sa_ref[:, cols(cb)], 0, cb)
        if cb >= second_from:
            sb_ref[:, cols(cb)] = _dot(k_last, qt_ref[:, cols(cb)])
        ms[cb] = absorb(cb, ms[cb], s, jnp.max(s, axis=0, keepdims=True), vt1)
    vt1 = with_ones(vt_ref[j0 + 1])
    for cb in range(second_from, ncb):
        s = masked(sb_ref[:, cols(cb)], 1, cb)
        ms[cb] = absorb(cb, ms[cb], s, jnp.max(s, axis=0, keepdims=True), vt1)

    for cb in range(ncb):
        o = acc_ref[0:D_V, cols(cb)] / acc_ref[D_V:D_V + 1, cols(cb)]
        o_ref[cb * qb:(cb + 1) * qb, :] = o.T.astype(BF16)


def _attention(qt, k, vt, km, vmt, *, tq):
    b, h, _, s = qt.shape
    grid = (b, h, s // tq)
    nkb = vt.shape[2]
    return pl.pallas_call(
        functools.partial(_attn_kernel, tq=tq),
        grid=grid,
        in_specs=[
            pl.BlockSpec((None, None, QK_PAD, tq), lambda bi, hi, i: (bi, hi, 0, i)),
            pl.BlockSpec((None, None, s, QK_PAD), lambda bi, hi, i: (bi, hi, 0, 0)),
            pl.BlockSpec((None, None, nkb, D_V, _KV_BLOCK), lambda bi, hi, i: (bi, hi, 0, 0, 0)),
            pl.BlockSpec((None, LANES, QK_PAD), lambda bi, hi, i: (hi, 0, 0)),
            pl.BlockSpec((None, D_V, LANES), lambda bi, hi, i: (hi, 0, 0)),
        ],
        out_specs=pl.BlockSpec((None, tq, D_V), lambda bi, hi, i: (bi, i, hi)),
        out_shape=jax.ShapeDtypeStruct((b, s, h * D_V), BF16),
        scratch_shapes=[pltpu.VMEM((D_V + _DENOM_ROWS, tq), F32), pltpu.VMEM((_KV_BLOCK, tq), F32),
                        pltpu.VMEM((_KV_BLOCK, tq), F32)],
        compiler_params=pltpu.CompilerParams(
            dimension_semantics=("arbitrary", "arbitrary", "arbitrary"), vmem_limit_bytes=VMEM_LIMIT),
        name="attn",
    )(qt, k, vt, km, vmt)


def _outproj_kernel(xscale_ref, x_ref, oa_ref, cn_ref, ln0g_ref, ln0b_ref, ag_ref, wo_ref, ln1g_ref, ln1b_ref,
                    wq_ref, sk_ref, h1_out, h1q_out, st_out):
    h0 = _layer_norm(x_ref[...], ln0g_ref[...], ln0b_ref[...])
    an = _rms_norm(oa_ref[...].astype(F32), ag_ref[...]).astype(BF16)
    mixed = _dot(an, wo_ref[0:ATTN_DIM, :]) + _dot(cn_ref[...], wo_ref[ATTN_DIM:, :])
    h1 = _layer_norm(DEEPNORM_ALPHA * h0 + mixed, ln1g_ref[...], ln1b_ref[...])
    h1_out[...] = h1
    h1q_out[...] = (h1 * xscale_ref[0]).astype(F8)
    h1b = h1.astype(BF16)
    pq = _dot(h1b, wq_ref[...])
    for hc in range(PEER_HEADS * 2):
        pqs = pq[:, hc * D_HALF:(hc + 1) * D_HALF].astype(BF16)
        st_out[hc * N_KEYS:(hc + 1) * N_KEYS, :] = _dot_nt(sk_ref[hc % 2], pqs)


def _outproj(xscale, x, oa, cn, ln0g, ln0b, ag, wo, ln1g, ln1b, wq, sk, *, tm):
    b, s, d = x.shape
    grid = (b, s // tm)
    nsc = PEER_HEADS * 2 * N_KEYS
    tok = lambda w: pl.BlockSpec((None, tm, w), lambda bi, i: (bi, i, 0))
    return pl.pallas_call(
        _outproj_kernel,
        grid=grid,
        in_specs=[pl.BlockSpec(memory_space=pltpu.SMEM), tok(d), tok(oa.shape[2]), tok(cn.shape[2]),
                  _const_spec(ln0g.shape), _const_spec(ln0b.shape), _const_spec(ag.shape), _const_spec(wo.shape),
                  _const_spec(ln1g.shape), _const_spec(ln1b.shape), _const_spec(wq.shape), _const_spec(sk.shape)],
        out_specs=(tok(d), tok(d), pl.BlockSpec((None, nsc, tm), lambda bi, i: (bi, 0, i))),
        out_shape=(jax.ShapeDtypeStruct((b, s, d), F32), jax.ShapeDtypeStruct((b, s, d), F8),
                   jax.ShapeDtypeStruct((b, nsc, s), F32)),
        compiler_params=pltpu.CompilerParams(
            dimension_semantics=("arbitrary", "arbitrary"), vmem_limit_bytes=VMEM_LIMIT),
        name="outproj",
    )(xscale, x, oa, cn, ln0g, ln0b, ag, wo, ln1g, ln1b, wq, sk)


_BIG_ID = 1 << 20


_SUBLANES = 8


def _top_keys(s_ref, base, n):
    w = s_ref.shape[1]
    nv = N_KEYS // _SUBLANES
    assert n <= nv
    sub = lax.broadcasted_iota(jnp.int32, (_SUBLANES, w), 0)
    vals = [s_ref[pl.ds(pl.multiple_of(base + r * _SUBLANES, _SUBLANES), _SUBLANES), :] for r in range(nv)]
    ids = [sub + r * _SUBLANES for r in range(nv)]
    for rnd in range(nv):
        for i in range(rnd % 2, nv - 1, 2):
            swap = vals[i + 1] > vals[i]
            hi, lo = jnp.maximum(vals[i], vals[i + 1]), jnp.minimum(vals[i], vals[i + 1])
            ids[i], ids[i + 1] = jnp.where(swap, ids[i + 1], ids[i]), jnp.where(swap, ids[i], ids[i + 1])
            vals[i], vals[i + 1] = hi, lo
    rows = lax.broadcasted_iota(jnp.int32, (n, w), 0)
    out_v = jnp.zeros((n, w), F32)
    out_i = jnp.zeros((n, w), jnp.int32)
    val_rows = []
    for t in range(n):
        m = jnp.max(vals[0], axis=0, keepdims=True)
        pick = jnp.min(jnp.where(vals[0] == m, ids[0], _BIG_ID), axis=0, keepdims=True)
        out_v = jnp.where(rows == t, m, out_v)
        out_i = jnp.where(rows == t, pick, out_i)
        val_rows.append(m)
        win = ids[0] == pick
        for r in range(n - 1 - t):
            vals[r] = jnp.where(win, vals[r + 1], vals[r])
            ids[r] = jnp.where(win, ids[r + 1], ids[r])
    return out_v, out_i, val_rows


def _top_pair_sums(v1, v2rows, n):
    w = v1.shape[1]
    half = n // 2
    assert half == _SUBLANES
    ra = lax.broadcasted_iota(jnp.int32, (half, w), 0)
    top = v1[0:half, :]
    bot = v1[half:, :] + v2rows[0]
    bot_id = (ra + half) * n
    vals = [jnp.where(ra < n // (b + 1), top + v2rows[b], -jnp.inf) for b in range(n)]
    ids = [ra * n + b for b in range(n)]
    rows = lax.broadcasted_iota(jnp.int32, (n, w), 0)
    out_v = jnp.zeros((n, w), F32)
    out_i = jnp.zeros((n, w), jnp.int32)
    for t in range(n):
        m = jnp.max(jnp.maximum(vals[0], bot), axis=0, keepdims=True)
        lowest = jnp.minimum(jnp.where(vals[0] == m, ids[0], _BIG_ID), jnp.where(bot == m, bot_id, _BIG_ID))
        pick = jnp.min(lowest, axis=0, keepdims=True)
        out_v = jnp.where(rows == t, m, out_v)
        out_i = jnp.where(rows == t, pick, out_i)
        bot = jnp.where(bot_id == pick, -jnp.inf, bot)
        win = ids[0] == pick
        for b in range(n - 1 - t):
            vals[b] = jnp.where(win, vals[b + 1], vals[b])
            ids[b] = jnp.where(win, ids[b + 1], ids[b])
    return out_v, out_i


def _route_kernel(st_ref, i1_out, i2_out, g_out):
    w = st_ref.shape[1]
    k = PEER_TOPK

    def head(h, _):
        base = pl.multiple_of(h * 2 * N_KEYS, 2 * N_KEYS)
        v1, i1, _ = _top_keys(st_ref, base, k)
        _, i2, v2rows = _top_keys(st_ref, base + N_KEYS, k)
        best, best_id = _top_pair_sums(v1, v2rows, k)
        a_sel = best_id >> _TOPK_SHIFT
        b_sel = best_id & (k - 1)
        e1 = jnp.zeros((k, w), jnp.int32)
        e2 = jnp.zeros((k, w), jnp.int32)
        for c in range(k):
            e1 = jnp.where(a_sel == c, i1[c:c + 1, :], e1)
            e2 = jnp.where(b_sel == c, i2[c:c + 1, :], e2)
        ex = jnp.exp(best - best[0:1, :])
        gate = ex / jnp.sum(ex, axis=0, keepdims=True)
        o = pl.multiple_of(h * k, k)
        i1_out[pl.ds(o, k), :] = e1
        i2_out[pl.ds(o, k), :] = e2
        g_out[pl.ds(o, k), :] = gate
        return 0

    lax.fori_loop(0, PEER_HEADS, head, 0, unroll=True)


def _route(st, *, tl):
    b, nsc, s = st.shape
    nj = PEER_HEADS * PEER_TOPK
    grid = (b, s // tl)
    ospec = pl.BlockSpec((None, nj, tl), lambda bi, i: (bi, 0, i))
    return pl.pallas_call(
        _route_kernel,
        grid=grid,
        in_specs=[pl.BlockSpec((None, nsc, tl), lambda bi, i: (bi, 0, i))],
        out_specs=(ospec, ospec, ospec),
        out_shape=(jax.ShapeDtypeStruct((b, nj, s), jnp.int32), jax.ShapeDtypeStruct((b, nj, s), jnp.int32),
                   jax.ShapeDtypeStruct((b, nj, s), F32)),
        compiler_params=pltpu.CompilerParams(dimension_semantics=("arbitrary", "arbitrary")),
        name="route",
    )(st)


_GROUP = 16
_PITCH = N_KEYS + 4


def _scatter_kernel(i1_ref, i2_ref, g_ref, gd_out, stage_a, stage_b):
    tg = i1_ref.shape[0]
    nj = i1_ref.shape[1]
    sub = lax.broadcasted_iota(jnp.int32, (N_KEYS, nj), 0)
    stages = (stage_a, stage_b)

    def fill(grp):
        stage = stages[grp % 2]
        for t in range(_GROUP):
            tok = grp * _GROUP + t
            r1 = i1_ref[tok:tok + 1, :]
            r2 = i2_ref[tok:tok + 1, :]
            gg = g_ref[tok:tok + 1, :]
            p1 = jnp.where(sub == r1, gg, 0.0).astype(BF16)
            p2 = jnp.where(sub == r2, 1.0, 0.0).astype(BF16)
            stage[t * _PITCH:t * _PITCH + N_KEYS, :] = _dot_nt(p1, p2)

    def drain(grp):
        stage = stages[grp % 2]
        for a in range(N_KEYS):
            rows = stage[pl.ds(a, _GROUP, stride=_PITCH), :]
            gd_out[grp * _GROUP:(grp + 1) * _GROUP, a * N_KEYS:(a + 1) * N_KEYS] = rows.astype(BF16)

    ngroups = tg // _GROUP
    fill(0)
    for grp in range(ngroups):
        if grp + 1 < ngroups:
            fill(grp + 1)
        drain(grp)


def _scatter(i1, i2, g, *, tg):
    t, nj = i1.shape
    ne = N_KEYS * N_KEYS
    spec = pl.BlockSpec((tg, nj), lambda i: (i, 0))
    return pl.pallas_call(
        _scatter_kernel,
        grid=(t // tg,),
        in_specs=[spec, spec, spec],
        out_specs=pl.BlockSpec((tg, ne), lambda i: (i, 0)),
        out_shape=jax.ShapeDtypeStruct((t, ne), BF16),
        scratch_shapes=[pltpu.VMEM((_GROUP * _PITCH, N_KEYS), F32), pltpu.VMEM((_GROUP * _PITCH, N_KEYS), F32)],
        compiler_params=pltpu.CompilerParams(dimension_semantics=("arbitrary",)),
        name="scatter",
    )(i1, i2, g)


def _peer_kernel(unscale_ref, xq_ref, h1_ref, u_ref, v_ref, gd_ref, ln2g_ref, ln2b_ref, o_ref):
    e = pl.program_id(1)

    @pl.when(e == 0)
    def _():
        o_ref[...] = jnp.zeros_like(o_ref)

    raw = _dot_nt(xq_ref[...], u_ref[...])
    w = 1.0 + lax.erf(raw * (unscale_ref[0] * 2.0 ** -0.5))
    hd = (raw * w * gd_ref[...].astype(F32)).astype(F8)
    o_ref[...] += _dot(hd, v_ref[...])

    @pl.when(e == pl.num_programs(1) - 1)
    def _():
        def chunk(c, carry):
            r = pl.ds(pl.multiple_of(c * _LN_ROWS, _LN_ROWS), _LN_ROWS)
            f = o_ref[r, :] * unscale_ref[1]
            o_ref[r, :] = _layer_norm(DEEPNORM_ALPHA * h1_ref[r, :] + f, ln2g_ref[...], ln2b_ref[...])
            return carry

        lax.fori_loop(0, o_ref.shape[0] // _LN_ROWS, chunk, 0)


def _peer(unscale, xq, h1, u, v, gd, ln2g, ln2b, *, tm, ec):
    t, d = xq.shape
    ne = v.shape[0]
    grid = (t // tm, ne // ec)
    return pl.pallas_call(
        _peer_kernel,
        grid=grid,
        in_specs=[
            pl.BlockSpec(memory_space=pltpu.SMEM),
            pl.BlockSpec((tm, d), lambda i, e: (i, 0)),
            pl.BlockSpec((tm, d), lambda i, e: (i, 0), pipeline_mode=pl.Buffered(1)),
            pl.BlockSpec((ec, d), lambda i, e: (e, 0)),
            pl.BlockSpec((ec, d), lambda i, e: (e, 0)),
            pl.BlockSpec((tm, ec), lambda i, e: (i, e)),
            _const_spec(ln2g.shape), _const_spec(ln2b.shape),
        ],
        out_specs=pl.BlockSpec((tm, d), lambda i, e: (i, 0)),
        out_shape=jax.ShapeDtypeStruct((t, d), F32),
        compiler_params=pltpu.CompilerParams(
            dimension_semantics=("arbitrary", "arbitrary"), vmem_limit_bytes=VMEM_LIMIT),
        name="peer",
    )(unscale, xq, h1, u, v, gd, ln2g, ln2b)


def _pow2_below(x):
    return jnp.exp2(jnp.floor(jnp.log2(x)))


def _peer_scales(ln1_g, ln1_b, peer_u, peer_v, d):
    tiny = jnp.float32(1e-30)
    h_elem = jnp.max(jnp.sqrt(float(d)) * jnp.abs(ln1_g) + jnp.abs(ln1_b))
    h_norm = jnp.sqrt(float(d)) * jnp.max(jnp.abs(ln1_g)) + jnp.sqrt(jnp.sum(ln1_b * ln1_b))
    u_row = jnp.sqrt(jnp.max(jnp.sum(peer_u * peer_u, axis=1)))
    u_max = jnp.max(jnp.max(jnp.abs(peer_u), axis=1))
    sx = _pow2_below(_F8_TARGET / jnp.maximum(h_elem, tiny))
    su = _pow2_below(_F8_TARGET / jnp.maximum(u_max, tiny))
    sv = _pow2_below(_F8_TARGET / jnp.maximum(jnp.max(jnp.abs(peer_v)), tiny))
    sh = _pow2_below(_F8_TARGET / jnp.maximum(h_norm * u_row, tiny))
    return sx, su, sv, sh


def _rope_tables(first_pos, n):
    inv = 1.0 / (ROPE_THETA ** (jnp.arange(0, D_ROPE, 2, dtype=F32) / D_ROPE))
    ang = (first_pos + jnp.arange(n, dtype=F32))[:, None] * inv[None, :]
    cos, sin = jnp.cos(ang), jnp.sin(ang)
    z = jnp.zeros((n, LANES - D_ROPE), F32)
    return jnp.concatenate([cos, cos, z], axis=1), jnp.concatenate([-sin, sin, z], axis=1)


def _swap_halves(w):
    half = w.shape[-1] // 2
    return jnp.concatenate([w[..., half:], w[..., :half]], axis=-1)


def _pick(n, prefs):
    for p in prefs:
        if n % p == 0:
            return p
    raise ValueError(f"no tile in {prefs} divides {n}")


def kernel(x, meta_tokens, ln0_g, ln0_b, w_in, q_norm_g, kv_norm_g, w_uq, w_ukv, conv_w, attn_out_g, conv_out_g,
           w_o, ln1_g, ln1_b, peer_w_query, peer_sub_keys, peer_u, peer_v, ln2_g, ln2_b):
    b, s, d = x.shape
    assert w_in.shape[0] == DEPTH and s % _Q_TILE == 0 and conv_w.shape[1] == CONV_WIDTH
    conv_dim = conv_w.shape[2]
    h = MLA_HEADS
    row = lambda a: a.reshape(1, -1).astype(F32)

    wi = w_in[0]
    o_kr, o_b = Q_RANK + KV_RANK, Q_RANK + KV_RANK + D_ROPE
    w_kr = wi[:, o_kr:o_b]
    zpad = jnp.zeros((d, LANES - D_ROPE), F32)
    win = jnp.concatenate(
        [wi[:, :o_kr], w_kr, zpad, _swap_halves(w_kr), zpad, wi[:, o_b:]], axis=1).astype(BF16)
    wq3 = w_uq[0].reshape(Q_RANK, h, D_QK)
    zq = jnp.zeros((Q_RANK, h, QK_PAD - D_QK), F32)
    wqat = jnp.concatenate([wq3, zq], axis=2).reshape(Q_RANK, h * QK_PAD).T.astype(BF16)
    wqbt = jnp.concatenate([_swap_halves(wq3[:, :, D_NOPE:]), zq], axis=2).reshape(Q_RANK, h * LANES).T.astype(BF16)
    wkv3 = w_ukv[0].reshape(KV_RANK, h, D_NOPE + D_V)
    wk = wkv3[:, :, :D_NOPE].reshape(KV_RANK, h * D_NOPE).astype(BF16)
    wvt = wkv3[:, :, D_NOPE:].reshape(KV_RANK, h * D_V).T.astype(BF16)
    wo = w_o[0].astype(BF16)
    wpq = peer_w_query[0].astype(BF16)
    sk = peer_sub_keys[0].astype(BF16)
    sx, su, sv, sh = _peer_scales(ln1_g[0], ln1_b[0], peer_u[0], peer_v[0], d)
    pu = (peer_u[0] * su).astype(F8)
    pv = (peer_v[0] * sv).astype(F8)
    unscale = jnp.stack([1.0 / (sx * su), 1.0 / (sh * sv)]).astype(F32)

    inproj_w = (row(ln0_g), row(ln0_b), win, row(q_norm_g[0]), row(kv_norm_g[0]), wqat, wqbt, wk, wvt,
                conv_w[0].astype(F32), row(conv_out_g[0]))

    t1m, t2m = _rope_tables(0, N_META)
    _, km, vmt, _, utail = _inproj(meta_tokens[None].astype(F32), t1m, t2m, t1m.T, t2m.T,
                                   jnp.zeros((8, conv_dim), F32), *inproj_w, tm=N_META)
    km = jnp.pad(km[0], ((0, 0), (0, LANES - N_META), (0, 0)))
    vmt = jnp.pad(vmt[0, :, 0], ((0, 0), (0, 0), (0, LANES - N_META)))

    t1, t2 = _rope_tables(N_META, s)
    qt, k, vt, convn, _ = _inproj(x, t1, t2, t1.T, t2.T, utail[0], *inproj_w, tm=_KV_BLOCK)
    oa = _attention(qt, k, vt, km, vmt, tq=_Q_TILE)
    h1, h1q, st = _outproj(sx.reshape(1), x, oa, convn, row(ln0_g), row(ln0_b), row(attn_out_g[0]), wo,
                           row(ln1_g[0]), row(ln1_b[0]), wpq, sk, tm=_pick(s, (256, 128)))
    i1, i2, g = _route(st, tl=LANES)
    tok = lambda a: jnp.swapaxes(a, 1, 2).reshape(b * s, a.shape[1])
    gd = _scatter(tok(i1), tok(i2), tok(g) * (sh * 0.5 * unscale[0]), tg=_pick(b * s, (256, 128, 64)))
    out = _peer(unscale, h1q.reshape(b * s, d), h1.reshape(b * s, d), pu, pv, gd, row(ln2_g[0]), row(ln2_b[0]),
                tm=_pick(b * s, (1024, 512)), ec=_PEER_CHUNK)
    return out.reshape(b, s, d)
```
